```python
import math
import jax
import jax.numpy as jnp
from jax import lax
import numpy as np

D_MODEL = 1024
BATCH = 8
SEQ = 8192
DEPTH = 2

GRID_W = 64
CTX_LEN = 256

N_MIXERS = 2
MIXER_DELTA = 0
MIXER_HYENA = 1
N_DELTA_LAYERS = (DEPTH - MIXER_DELTA + N_MIXERS - 1) // N_MIXERS
N_HYENA_LAYERS = (DEPTH - MIXER_HYENA + N_MIXERS - 1) // N_MIXERS

EPS = 1e-6

DN_HEAD_DIM = 128
DN_HEADS = D_MODEL // DN_HEAD_DIM
DN_INNER = DN_HEADS * DN_HEAD_DIM
DN_CONV = 5
DN_CHUNK = 64
DN_CHUNK_LOG2 = 6
DN_PROJ = 4 * DN_INNER + 4 * DN_HEADS

HY_WIDTH = D_MODEL
HY_SHORT = 3
HY_BANDS = 16
HY_EMB = 1 + 2 * HY_BANDS
HY_FF = 64
HY_FAST = 0.3
HY_SLOW = 1.5
HY_TARGET = 1e-2

PEER_HEADS = 8
PEER_NKEYS = 128
PEER_EXPERTS = PEER_NKEYS * PEER_NKEYS
PEER_DK = 256
PEER_TOPK = 16
PEER_BLOCK = 128

kernel_name = "hybrid_deltanet_hyena_peer_dit"


def _rmsnorm(x, g):
    xf = x.astype(jnp.float32)
    y = xf * lax.rsqrt(jnp.mean(xf * xf, axis=-1, keepdims=True) + EPS)
    return (y * g.astype(jnp.float32)).astype(x.dtype)


def _modulate(xn, shift, scale):
    return xn * (1 + scale) + shift


def _l2norm(x):
    xf = x.astype(jnp.float32)
    return xf * lax.rsqrt(jnp.sum(xf * xf, axis=-1, keepdims=True) + EPS)


def _rev(t):
    return jnp.flip(t, axis=2)


def _short_conv(x, w, n_rows):
    B, L, C = x.shape
    K = w.shape[0]
    xr = x.reshape(B * n_rows, L // n_rows, C)
    y = lax.conv_general_dilated(
        xr, w.astype(x.dtype)[:, None, :], window_strides=(1,),
        padding=[((K - 1) // 2, (K - 1) // 2)],
        dimension_numbers=('NWC', 'WIO', 'NWC'), feature_group_count=C)
    return y.reshape(B, L, C)


def _dn_inputs(hn, n_rows, w_in, conv_w, a_log, dt_bias):
    B, L, _ = hn.shape
    z = hn @ w_in
    qkv = jax.nn.silu(_short_conv(z[..., :3 * DN_INNER], conv_w, n_rows))
    gate = z[..., 3 * DN_INNER:4 * DN_INNER]
    ba = z[..., 4 * DN_INNER:].astype(jnp.float32).reshape(B, L, 2, 2, DN_HEADS)
    qkv = qkv.reshape(B, L, 3, DN_HEADS, DN_HEAD_DIM).transpose(2, 0, 3, 1, 4)
    q = _l2norm(qkv[0]) * (DN_HEAD_DIM ** -0.5)
    k = _l2norm(qkv[1])
    v = qkv[2].astype(jnp.float32)
    beta = jax.nn.sigmoid(ba[:, :, 0]).transpose(0, 2, 3, 1)
    g = (-jnp.exp(a_log.astype(jnp.float32)) *
         jax.nn.softplus(ba[:, :, 1] + dt_bias.astype(jnp.float32))).transpose(0, 2, 3, 1)
    return q, k, v, beta, g, gate


def _unit_lower_inverse(a):
    eye = jnp.eye(a.shape[-1], dtype=a.dtype)
    t = eye - a
    p = a
    for _ in range(DN_CHUNK_LOG2 - 1):
        p = p @ p
        t = t @ (eye + p)
    return t


def _gated_delta_chunked(k, v, beta, g, s0, q=None):
    f32 = jnp.float32
    with_out = q is not None
    B, H, L, DK = k.shape
    DV = v.shape[-1]
    C = DN_CHUNK
    n = L // C
    k = k.astype(f32).reshape(B, H, n, C, DK)
    v = v.astype(f32).reshape(B, H, n, C, DV)
    beta = beta.astype(f32).reshape(B, H, n, C)
    G = jnp.cumsum(g.astype(f32).reshape(B, H, n, C), axis=-1)
    incl = jnp.tril(jnp.ones((C, C), bool))
    strict = jnp.tril(jnp.ones((C, C), bool), -1)
    dmat = jnp.exp(jnp.where(incl, G[..., :, None] - G[..., None, :], -jnp.inf))
    kb = k * beta[..., None]
    a = jnp.where(strict, jnp.einsum('bhnid,bhnjd->bhnij', kb, k) * dmat, 0.0)
    t = _unit_lower_inverse(a)
    u = t @ (v * beta[..., None])
    w = t @ (kb * jnp.exp(G)[..., None])
    kd = k * jnp.exp(G[..., -1:] - G)[..., None]
    gl = jnp.exp(G[..., -1])
    xs = (jnp.moveaxis(w, 2, 0), jnp.moveaxis(u, 2, 0), jnp.moveaxis(kd, 2, 0), jnp.moveaxis(gl, 2, 0))
    if with_out:
        q = q.astype(f32).reshape(B, H, n, C, DK)
        qa = jnp.einsum('bhnid,bhnjd->bhnij', q, k) * dmat
        qg = q * jnp.exp(G)[..., None]
        xs = xs + (jnp.moveaxis(qg, 2, 0), jnp.moveaxis(qa, 2, 0))

    def step(s, inp):
        w_c, u_c, kd_c, gl_c = inp[:4]
        v_new = u_c - jnp.einsum('bhck,bhkv->bhcv', w_c, s)
        s_next = s * gl_c[..., None, None] + jnp.einsum('bhck,bhcv->bhkv', kd_c, v_new)
        if not with_out:
            return s_next, None
        qg_c, qa_c = inp[4:]
        o_c = jnp.einsum('bhck,bhkv->bhcv', qg_c, s) + jnp.einsum('bhij,bhjv->bhiv', qa_c, v_new)
        return s_next, o_c

    s_fin, o = lax.scan(step, s0.astype(f32), xs)
    if not with_out:
        return None, s_fin
    return jnp.moveaxis(o, 0, 2).reshape(B, H, L, DV), s_fin


def _gated_out(o, gate, g_norm, w_out):
    B, H, L, DV = o.shape
    o = jnp.swapaxes(o, 1, 2)
    on = o * lax.rsqrt(jnp.mean(o * o, axis=-1, keepdims=True) + EPS) * g_norm.astype(jnp.float32)
    y = on * jax.nn.silu(gate.astype(jnp.float32)).reshape(B, L, H, DV)
    return y.reshape(B, L, H * DV).astype(gate.dtype) @ w_out


def _deltanet(hn, hcn, n_rows, w_in, conv_w, a_log, dt_bias, o_norm_g, w_out, ctx_out):
    q, k, v, beta, g, gate = _dn_inputs(hn, n_rows, w_in, conv_w, a_log, dt_bias)
    qc, kc, vc, betac, gc, gatec = _dn_inputs(hcn, 1, w_in, conv_w, a_log, dt_bias)
    B = hn.shape[0]
    s0 = jnp.zeros((B, DN_HEADS, DN_HEAD_DIM, DN_HEAD_DIM), jnp.float32)
    oc_f, sc_f = _gated_delta_chunked(kc, vc, betac[:, 0], gc[:, 0], s0, qc if ctx_out else None)
    oc_b, sc_b = _gated_delta_chunked(_rev(kc), _rev(vc), _rev(betac[:, 1]), _rev(gc[:, 1]), s0,
                                      _rev(qc) if ctx_out else None)
    o_f, _ = _gated_delta_chunked(k, v, beta[:, 0], g[:, 0], sc_f, q)
    o_b, _ = _gated_delta_chunked(_rev(k), _rev(v), _rev(beta[:, 1]), _rev(g[:, 1]), sc_b, _rev(q))
    y = _gated_out(o_f + _rev(o_b), gate, o_norm_g, w_out)
    yc = _gated_out(oc_f + _rev(oc_b), gatec, o_norm_g, w_out) if ctx_out else None
    return y, yc


def _hyena_filters(L, w1, b1, w2, b2, w3, b3, w4, freq):
    f32 = jnp.float32
    D = w4.shape[-1] // 4
    pos = jnp.arange(L, dtype=f32)
    t = pos / max(L - 1, 1)
    ang = (2 * math.pi * pos / L)[:, None] * jnp.linspace(1e-4, HY_BANDS - 1, HY_BANDS, dtype=f32)[None]
    z = jnp.concatenate([t[:, None], jnp.cos(ang), -jnp.sin(ang)], axis=-1)
    fr = freq.astype(f32)
    hid = jnp.sin(fr * (z @ w1.astype(f32) + b1.astype(f32)))
    hid = jnp.sin(fr * (hid @ w2.astype(f32) + b2.astype(f32)))
    hid = jnp.sin(fr * (hid @ w3.astype(f32) + b3.astype(f32)))
    h = hid @ w4.astype(f32)
    deltas = jnp.abs(jnp.linspace(math.log(HY_TARGET) / HY_FAST, math.log(HY_TARGET) / HY_SLOW, D, dtype=f32))
    decay = jnp.exp(-t[:, None] * deltas[None])
    return (h.reshape(L, 2, 2, D) * decay[:, None, None, :]).transpose(1, 2, 0, 3)


def _long_conv(u, hf, hb, d):
    B, L, D = u.shape
    filt2 = jnp.concatenate([hf, jnp.zeros((1, D), jnp.float32), jnp.flip(hb[1:], axis=0)], axis=0)
    uf = jnp.fft.rfft(u.astype(jnp.float32), n=2 * L, axis=1)
    y = jnp.fft.irfft(uf * jnp.fft.rfft(filt2, axis=0)[None], n=2 * L, axis=1)[:, :L]
    return (y + u.astype(jnp.float32) * d.astype(jnp.float32)).astype(u.dtype)


def _hyena(hn, n_rows, w_in, conv_w, conv_b, f_w1, f_b1, f_w2, f_b2, f_w3, f_b3, f_w4, freq, skip, w_out):
    B, L, _ = hn.shape
    z = _short_conv(hn @ w_in, conv_w, n_rows) + conv_b.astype(hn.dtype)
    v, x1, x2 = jnp.split(z, 3, axis=-1)
    filt = _hyena_filters(L, f_w1, f_b1, f_w2, f_b2, f_w3, f_b3, f_w4, freq)
    z1 = x1 * _long_conv(v, filt[0, 0], filt[0, 1], skip[0])
    z2 = x2 * _long_conv(z1, filt[1, 0], filt[1, 1], skip[1])
    return z2 @ w_out


def _peer(x, w_q, sub_keys, u_tab, v_tab):
    B, L, D = x.shape
    T = B * L
    xt = x.reshape(T, D)
    q = (xt @ w_q).astype(jnp.float32).reshape(T, PEER_HEADS, 2, PEER_DK // 2)
    s = jnp.einsum('thpd,hpkd->thpk', q, sub_keys.astype(jnp.float32))
    s1, i1 = lax.top_k(s[:, :, 0], PEER_TOPK)
    s2, i2 = lax.top_k(s[:, :, 1], PEER_TOPK)
    n_cand = PEER_TOPK * PEER_TOPK
    cand_s = (s1[..., :, None] + s2[..., None, :]).reshape(T, PEER_HEADS, n_cand)
    cand_i = (i1[..., :, None] * PEER_NKEYS + i2[..., None, :]).reshape(T, PEER_HEADS, n_cand)
    top_s, pos = lax.top_k(cand_s, PEER_TOPK)
    idx = jnp.take_along_axis(cand_i, pos, axis=-1)
    gate = jax.nn.softmax(top_s, axis=-1)
    nb = T // PEER_BLOCK

    def expert_block(args):
        xb, ib, gb = args
        hid = jnp.einsum('td,thkd->thk', xb, u_tab[ib]).astype(jnp.float32)
        coef = (jax.nn.gelu(hid, approximate=False) * gb).astype(xb.dtype)
        return jnp.einsum('thk,thkd->td', coef, v_tab[ib])

    y = lax.map(expert_block, (xt.reshape(nb, PEER_BLOCK, D),
                               idx.reshape(nb, PEER_BLOCK, PEER_HEADS, PEER_TOPK),
                               gate.reshape(nb, PEER_BLOCK, PEER_HEADS, PEER_TOPK)))
    return y.reshape(B, L, D)


def setup_inputs(seed: int = 0) -> dict:
    key = jax.random.key(seed)
    ks = iter(jax.random.split(key, 32))
    f32 = jnp.float32
    D = D_MODEL

    def nrm(shape, scale):
        return jax.random.normal(next(ks), shape, f32) * scale

    x = nrm((BATCH, SEQ, D), 1.0)
    c = nrm((BATCH, D), 1.0)
    ctx = nrm((BATCH, CTX_LEN, D), 1.0)
    c_ctx = nrm((D,), 1.0)
    ada_w = nrm((DEPTH, D, 6 * D), 0.5 * D ** -0.5)
    ada_b = nrm((DEPTH, 6 * D), 0.02)
    norm_g = 1.0 + nrm((DEPTH, 2, D), 0.02)
    dn_w_in = nrm((N_DELTA_LAYERS, D, DN_PROJ), D ** -0.5)
    dn_conv_w = nrm((N_DELTA_LAYERS, DN_CONV, 3 * DN_INNER), DN_CONV ** -0.5)
    dn_a_log = jnp.log(jax.random.uniform(next(ks), (N_DELTA_LAYERS, 2, DN_HEADS), f32, 1.0, 16.0))
    dt = jnp.exp(jax.random.uniform(next(ks), (N_DELTA_LAYERS, 2, DN_HEADS), f32,
                                    math.log(1e-3), math.log(1e-1)))
    dn_dt_bias = dt + jnp.log(-jnp.expm1(-dt))
    dn_norm_g = 1.0 + nrm((N_DELTA_LAYERS, DN_HEAD_DIM), 0.02)
    dn_w_out = nrm((N_DELTA_LAYERS, DN_INNER, D), DN_INNER ** -0.5)
    hy_w_in = nrm((N_HYENA_LAYERS, D, 3 * HY_WIDTH), D ** -0.5)
    hy_conv_w = nrm((N_HYENA_LAYERS, HY_SHORT, 3 * HY_WIDTH), HY_SHORT ** -0.5)
    hy_conv_b = nrm((N_HYENA_LAYERS, 3 * HY_WIDTH), 0.02)
    hy_f_w1 = nrm((N_HYENA_LAYERS, HY_EMB, HY_FF), HY_EMB ** -0.5)
    hy_f_b1 = nrm((N_HYENA_LAYERS, HY_FF), 0.02)
    hy_f_w2 = nrm((N_HYENA_LAYERS, HY_FF, HY_FF), HY_FF ** -0.5)
    hy_f_b2 = nrm((N_HYENA_LAYERS, HY_FF), 0.02)
    hy_f_w3 = nrm((N_HYENA_LAYERS, HY_FF, HY_FF), HY_FF ** -0.5)
    hy_f_b3 = nrm((N_HYENA_LAYERS, HY_FF), 0.02)
    hy_f_w4 = nrm((N_HYENA_LAYERS, HY_FF, 4 * HY_WIDTH), 0.02 * HY_FF ** -0.5)
    hy_freq = 1.0 + nrm((N_HYENA_LAYERS, HY_FF), 0.02)
    hy_skip = nrm((N_HYENA_LAYERS, 2, HY_WIDTH), 1.0)
    hy_w_out = nrm((N_HYENA_LAYERS, HY_WIDTH, D), HY_WIDTH ** -0.5)
    peer_w_q = nrm((DEPTH, D, PEER_HEADS * PEER_DK), D ** -0.5)
    peer_keys = nrm((DEPTH, PEER_HEADS, 2, PEER_NKEYS, PEER_DK // 2), (PEER_DK // 2) ** -0.5)
    peer_u = nrm((DEPTH, PEER_EXPERTS, D), D ** -0.5)
    peer_v = nrm((DEPTH, PEER_EXPERTS, D), PEER_HEADS ** -0.5)
    final_g = 1.0 + nrm((D,), 0.02)
    return {"x": x, "c": c, "ctx": ctx, "c_ctx": c_ctx,
            "ada_w": ada_w, "ada_b": ada_b, "norm_g": norm_g,
            "dn_w_in": dn_w_in, "dn_conv_w": dn_conv_w, "dn_a_log": dn_a_log, "dn_dt_bias": dn_dt_bias,
            "dn_norm_g": dn_norm_g, "dn_w_out": dn_w_out,
            "hy_w_in": hy_w_in, "hy_conv_w": hy_conv_w, "hy_conv_b": hy_conv_b,
            "hy_f_w1": hy_f_w1, "hy_f_b1": hy_f_b1, "hy_f_w2": hy_f_w2, "hy_f_b2": hy_f_b2,
            "hy_f_w3": hy_f_w3, "hy_f_b3": hy_f_b3, "hy_f_w4": hy_f_w4, "hy_freq": hy_freq,
            "hy_skip": hy_skip, "hy_w_out": hy_w_out,
            "peer_w_q": peer_w_q, "peer_keys": peer_keys, "peer_u": peer_u, "peer_v": peer_v,
            "final_g": final_g}


def reference(x, c, ctx, c_ctx, ada_w, ada_b, norm_g,
              dn_w_in, dn_conv_w, dn_a_log, dn_dt_bias, dn_norm_g, dn_w_out,
              hy_w_in, hy_conv_w, hy_conv_b, hy_f_w1, hy_f_b1, hy_f_w2, hy_f_b2,
              hy_f_w3, hy_f_b3, hy_f_w4, hy_freq, hy_skip, hy_w_out,
              peer_w_q, peer_keys, peer_u, peer_v, final_g):
    f32 = jnp.float32
    B, L, D = x.shape
    rows = L // GRID_W
    silu_c = jax.nn.silu(c.astype(f32))
    silu_cc = jax.nn.silu(c_ctx.astype(f32))
    h, hc = x, ctx
    for i in range(DEPTH):
        kind = i % N_MIXERS
        j = i // N_MIXERS
        ctx_out = any(l % N_MIXERS == MIXER_DELTA for l in range(i + 1, DEPTH))
        ctx_in = kind == MIXER_DELTA or ctx_out
        w_ada = ada_w[i].astype(f32)
        b_ada = ada_b[i].astype(f32)
        mod = (silu_c @ w_ada + b_ada).astype(h.dtype).reshape(B, 6, 1, D)
        sh1, sc1, gt1, sh2, sc2, gt2 = [mod[:, m] for m in range(6)]
        hn = _modulate(_rmsnorm(h, norm_g[i, 0]), sh1, sc1)
        hcn = None
        if ctx_in:
            modc = (silu_cc @ w_ada + b_ada).astype(hc.dtype).reshape(6, D)
            hcn = _modulate(_rmsnorm(hc, norm_g[i, 0]), modc[0], modc[1])
        if kind == MIXER_DELTA:
            y, yc = _deltanet(hn, hcn, rows, dn_w_in[j], dn_conv_w[j], dn_a_log[j], dn_dt_bias[j],
                              dn_norm_g[j], dn_w_out[j], ctx_out)
        else:
            hy_args = (hy_w_in[j], hy_conv_w[j], hy_conv_b[j], hy_f_w1[j], hy_f_b1[j], hy_f_w2[j],
                       hy_f_b2[j], hy_f_w3[j], hy_f_b3[j], hy_f_w4[j], hy_freq[j], hy_skip[j], hy_w_out[j])
            y = _hyena(hn, rows, *hy_args)
            yc = _hyena(hcn, 1, *hy_args) if ctx_out else None
        h = h + gt1 * y
        peer_args = (peer_w_q[i], peer_keys[i], peer_u[i], peer_v[i])
        h = h + gt2 * _peer(_modulate(_rmsnorm(h, norm_g[i, 1]), sh2, sc2), *peer_args)
        if ctx_out:
            hc = hc + modc[2] * yc
            hc = hc + modc[5] * _peer(_modulate(_rmsnorm(hc, norm_g[i, 1]), modc[3], modc[4]), *peer_args)
    return _rmsnorm(h, final_g)
```

```python
import math
import jax
import jax.numpy as jnp
from jax import lax
from jax.experimental import pallas as pl
from jax.experimental.pallas import tpu as pltpu

D_MODEL = 1024
BATCH = 8
SEQ = 8192
DEPTH = 2

GRID_W = 64
CTX_LEN = 256

N_MIXERS = 2
MIXER_DELTA = 0
MIXER_HYENA = 1

EPS = 1e-6

DN_HEAD_DIM = 128
DN_HEADS = D_MODEL // DN_HEAD_DIM
DN_INNER = DN_HEADS * DN_HEAD_DIM
DN_CONV = 5
DN_CHUNK = 64
DN_CHUNK_LOG2 = 6
DN_PROJ = 4 * DN_INNER + 4 * DN_HEADS

HY_WIDTH = D_MODEL
HY_SHORT = 3
HY_BANDS = 16
HY_EMB = 1 + 2 * HY_BANDS
HY_FF = 64
HY_FAST = 0.3
HY_SLOW = 1.5
HY_TARGET = 1e-2

PEER_HEADS = 8
PEER_NKEYS = 128
PEER_EXPERTS = PEER_NKEYS * PEER_NKEYS
PEER_DK = 256
PEER_TOPK = 16
PEER_BLOCK = 128


def _rmsnorm(x, g):
    xf = x.astype(jnp.float32)
    y = xf * lax.rsqrt(jnp.mean(xf * xf, axis=-1, keepdims=True) + EPS)
    return (y * g.astype(jnp.float32)).astype(x.dtype)


def _modulate(xn, shift, scale):
    return xn * (1 + scale) + shift


def _l2norm(x):
    xf = x.astype(jnp.float32)
    return xf * lax.rsqrt(jnp.sum(xf * xf, axis=-1, keepdims=True) + EPS)


def _rev(t):
    return jnp.flip(t, axis=2)


def _short_conv(x, w, n_rows):
    B, L, C = x.shape
    K = w.shape[0]
    xr = x.reshape(B * n_rows, L // n_rows, C)
    y = lax.conv_general_dilated(
        xr, w.astype(x.dtype)[:, None, :], window_strides=(1,),
        padding=[((K - 1) // 2, (K - 1) // 2)],
        dimension_numbers=('NWC', 'WIO', 'NWC'), feature_group_count=C)
    return y.reshape(B, L, C)


def _dn_inputs(hn, n_rows, w_in, conv_w, a_log, dt_bias):
    B, L, _ = hn.shape
    z = hn @ w_in
    qkv = jax.nn.silu(_short_conv(z[..., :3 * DN_INNER], conv_w, n_rows))
    gate = z[..., 3 * DN_INNER:4 * DN_INNER]
    ba = z[..., 4 * DN_INNER:].astype(jnp.float32).reshape(B, L, 2, 2, DN_HEADS)
    qkv = qkv.reshape(B, L, 3, DN_HEADS, DN_HEAD_DIM).transpose(2, 0, 3, 1, 4)
    q = _l2norm(qkv[0]) * (DN_HEAD_DIM ** -0.5)
    k = _l2norm(qkv[1])
    v = qkv[2].astype(jnp.float32)
    beta = jax.nn.sigmoid(ba[:, :, 0]).transpose(0, 2, 3, 1)
    g = (-jnp.exp(a_log.astype(jnp.float32)) *
         jax.nn.softplus(ba[:, :, 1] + dt_bias.astype(jnp.float32))).transpose(0, 2, 3, 1)
    return q, k, v, beta, g, gate


def _unit_lower_inverse(a):
    eye = jnp.eye(a.shape[-1], dtype=a.dtype)
    t = eye - a
    p = a
    for _ in range(DN_CHUNK_LOG2 - 1):
        p = p @ p
        t = t @ (eye + p)
    return t


def _gated_delta_chunked(k, v, beta, g, s0, q=None):
    f32 = jnp.float32
    with_out = q is not None
    B, H, L, DK = k.shape
    DV = v.shape[-1]
    C = DN_CHUNK
    n = L // C
    k = k.astype(f32).reshape(B, H, n, C, DK)
    v = v.astype(f32).reshape(B, H, n, C, DV)
    beta = beta.astype(f32).reshape(B, H, n, C)
    G = jnp.cumsum(g.astype(f32).reshape(B, H, n, C), axis=-1)
    incl = jnp.tril(jnp.ones((C, C), bool))
    strict = jnp.tril(jnp.ones((C, C), bool), -1)
    dmat = jnp.exp(jnp.where(incl, G[..., :, None] - G[..., None, :], -jnp.inf))
    kb = k * beta[..., None]
    a = jnp.where(strict, jnp.einsum('bhnid,bhnjd->bhnij', kb, k) * dmat, 0.0)
    t = _unit_lower_inverse(a)
    u = t @ (v * beta[..., None])
    w = t @ (kb * jnp.exp(G)[..., None])
    kd = k * jnp.exp(G[..., -1:] - G)[..., None]
    gl = jnp.exp(G[..., -1])
    xs = (jnp.moveaxis(w, 2, 0), jnp.moveaxis(u, 2, 0), jnp.moveaxis(kd, 2, 0), jnp.moveaxis(gl, 2, 0))
    if with_out:
        q = q.astype(f32).reshape(B, H, n, C, DK)
        qa = jnp.einsum('bhnid,bhnjd->bhnij', q, k) * dmat
        qg = q * jnp.exp(G)[..., None]
        xs = xs + (jnp.moveaxis(qg, 2, 0), jnp.moveaxis(qa, 2, 0))

    def step(s, inp):
        w_c, u_c, kd_c, gl_c = inp[:4]
        v_new = u_c - jnp.einsum('bhck,bhkv->bhcv', w_c, s)
        s_next = s * gl_c[..., None, None] + jnp.einsum('bhck,bhcv->bhkv', kd_c, v_new)
        if not with_out:
            return s_next, None
        qg_c, qa_c = inp[4:]
        o_c = jnp.einsum('bhck,bhkv->bhcv', qg_c, s) + jnp.einsum('bhij,bhjv->bhiv', qa_c, v_new)
        return s_next, o_c

    s_fin, o = lax.scan(step, s0.astype(f32), xs)
    if not with_out:
        return None, s_fin
    return jnp.moveaxis(o, 0, 2).reshape(B, H, L, DV), s_fin


def _gated_out(o, gate, g_norm, w_out):
    B, H, L, DV = o.shape
    o = jnp.swapaxes(o, 1, 2)
    on = o * lax.rsqrt(jnp.mean(o * o, axis=-1, keepdims=True) + EPS) * g_norm.astype(jnp.float32)
    y = on * jax.nn.silu(gate.astype(jnp.float32)).reshape(B, L, H, DV)
    return y.reshape(B, L, H * DV).astype(gate.dtype) @ w_out


def _deltanet(hn, hcn, n_rows, w_in, conv_w, a_log, dt_bias, o_norm_g, w_out, ctx_out):
    q, k, v, beta, g, gate = _dn_inputs(hn, n_rows, w_in, conv_w, a_log, dt_bias)
    qc, kc, vc, betac, gc, gatec = _dn_inputs(hcn, 1, w_in, conv_w, a_log, dt_bias)
    B = hn.shape[0]
    s0 = jnp.zeros((B, DN_HEADS, DN_HEAD_DIM, DN_HEAD_DIM), jnp.float32)
    oc_f, sc_f = _gated_delta_chunked(kc, vc, betac[:, 0], gc[:, 0], s0, qc if ctx_out else None)
    oc_b, sc_b = _gated_delta_chunked(_rev(kc), _rev(vc), _rev(betac[:, 1]), _rev(gc[:, 1]), s0,
                                      _rev(qc) if ctx_out else None)
    o_f, _ = _gated_delta_chunked(k, v, beta[:, 0], g[:, 0], sc_f, q)
    o_b, _ = _gated_delta_chunked(_rev(k), _rev(v), _rev(beta[:, 1]), _rev(g[:, 1]), sc_b, _rev(q))
    y = _gated_out(o_f + _rev(o_b), gate, o_norm_g, w_out)
    yc = _gated_out(oc_f + _rev(oc_b), gatec, o_norm_g, w_out) if ctx_out else None
    return y, yc


def _hyena_filters(L, w1, b1, w2, b2, w3, b3, w4, freq):
    f32 = jnp.float32
    D = w4.shape[-1] // 4
    pos = jnp.arange(L, dtype=f32)
    t = pos / max(L - 1, 1)
    ang = (2 * math.pi * pos / L)[:, None] * jnp.linspace(1e-4, HY_BANDS - 1, HY_BANDS, dtype=f32)[None]
    z = jnp.concatenate([t[:, None], jnp.cos(ang), -jnp.sin(ang)], axis=-1)
    fr = freq.astype(f32)
    hid = jnp.sin(fr * (z @ w1.astype(f32) + b1.astype(f32)))
    hid = jnp.sin(fr * (hid @ w2.astype(f32) + b2.astype(f32)))
    hid = jnp.sin(fr * (hid @ w3.astype(f32) + b3.astype(f32)))
    h = hid @ w4.astype(f32)
    deltas = jnp.abs(jnp.linspace(math.log(HY_TARGET) / HY_FAST, math.log(HY_TARGET) / HY_SLOW, D, dtype=f32))
    decay = jnp.exp(-t[:, None] * deltas[None])
    return (h.reshape(L, 2, 2, D) * decay[:, None, None, :]).transpose(1, 2, 0, 3)


def _long_conv(u, hf, hb, d):
    B, L, D = u.shape
    filt2 = jnp.concatenate([hf, jnp.zeros((1, D), jnp.float32), jnp.flip(hb[1:], axis=0)], axis=0)
    uf = jnp.fft.rfft(u.astype(jnp.float32), n=2 * L, axis=1)
    y = jnp.fft.irfft(uf * jnp.fft.rfft(filt2, axis=0)[None], n=2 * L, axis=1)[:, :L]
    return (y + u.astype(jnp.float32) * d.astype(jnp.float32)).astype(u.dtype)


def _hyena(hn, n_rows, w_in, conv_w, conv_b, f_w1, f_b1, f_w2, f_b2, f_w3, f_b3, f_w4, freq, skip, w_out):
    B, L, _ = hn.shape
    z = _short_conv(hn @ w_in, conv_w, n_rows) + conv_b.astype(hn.dtype)
    v, x1, x2 = jnp.split(z, 3, axis=-1)
    filt = _hyena_filters(L, f_w1, f_b1, f_w2, f_b2, f_w3, f_b3, f_w4, freq)
    z1 = x1 * _long_conv(v, filt[0, 0], filt[0, 1], skip[0])
    z2 = x2 * _long_conv(z1, filt[1, 0], filt[1, 1], skip[1])
    return z2 @ w_out


def _peer(x, w_q, sub_keys, u_tab, v_tab):
    B, L, D = x.shape
    T = B * L
    xt = x.reshape(T, D)
    q = (xt @ w_q).astype(jnp.float32).reshape(T, PEER_HEADS, 2, PEER_DK // 2)
    s = jnp.einsum('thpd,hpkd->thpk', q, sub_keys.astype(jnp.float32))
    s1, i1 = lax.top_k(s[:, :, 0], PEER_TOPK)
    s2, i2 = lax.top_k(s[:, :, 1], PEER_TOPK)
    n_cand = PEER_TOPK * PEER_TOPK
    cand_s = (s1[..., :, None] + s2[..., None, :]).reshape(T, PEER_HEADS, n_cand)
    cand_i = (i1[..., :, None] * PEER_NKEYS + i2[..., None, :]).reshape(T, PEER_HEADS, n_cand)
    top_s, pos = lax.top_k(cand_s, PEER_TOPK)
    idx = jnp.take_along_axis(cand_i, pos, axis=-1)
    gate = jax.nn.softmax(top_s, axis=-1)
    nb = T // PEER_BLOCK

    def expert_block(args):
        xb, ib, gb = args
        hid = jnp.einsum('td,thkd->thk', xb, u_tab[ib]).astype(jnp.float32)
        coef = (jax.nn.gelu(hid, approximate=False) * gb).astype(xb.dtype)
        return jnp.einsum('thk,thkd->td', coef, v_tab[ib])

    y = lax.map(expert_block, (xt.reshape(nb, PEER_BLOCK, D),
                               idx.reshape(nb, PEER_BLOCK, PEER_HEADS, PEER_TOPK),
                               gate.reshape(nb, PEER_BLOCK, PEER_HEADS, PEER_TOPK)))
    return y.reshape(B, L, D)


def _final_rmsnorm_kernel(h_ref, g_ref, o_ref):
    x = h_ref[...]
    y = x * lax.rsqrt(jnp.mean(x * x, axis=-1, keepdims=True) + EPS)
    o_ref[...] = y * g_ref[...]


def _final_rmsnorm(h, g):
    B, L, D = h.shape
    T = B * L
    tm = 1024
    out = pl.pallas_call(
        _final_rmsnorm_kernel,
        grid=(T // tm,),
        in_specs=[pl.BlockSpec((tm, D), lambda i: (i, 0)),
                  pl.BlockSpec((1, D), lambda i: (0, 0))],
        out_specs=pl.BlockSpec((tm, D), lambda i: (i, 0)),
        out_shape=jax.ShapeDtypeStruct((T, D), h.dtype),
        name="final_rmsnorm",
    )(h.reshape(T, D), g.reshape(1, D))
    return out.reshape(B, L, D)


def kernel(x, c, ctx, c_ctx, ada_w, ada_b, norm_g,
           dn_w_in, dn_conv_w, dn_a_log, dn_dt_bias, dn_norm_g, dn_w_out,
           hy_w_in, hy_conv_w, hy_conv_b, hy_f_w1, hy_f_b1, hy_f_w2, hy_f_b2,
           hy_f_w3, hy_f_b3, hy_f_w4, hy_freq, hy_skip, hy_w_out,
           peer_w_q, peer_keys, peer_u, peer_v, final_g):
    f32 = jnp.float32
    B, L, D = x.shape
    rows = L // GRID_W
    silu_c = jax.nn.silu(c.astype(f32))
    silu_cc = jax.nn.silu(c_ctx.astype(f32))
    h, hc = x, ctx
    for i in range(DEPTH):
        kind = i % N_MIXERS
        j = i // N_MIXERS
        ctx_out = any(l % N_MIXERS == MIXER_DELTA for l in range(i + 1, DEPTH))
        ctx_in = kind == MIXER_DELTA or ctx_out
        w_ada = ada_w[i].astype(f32)
        b_ada = ada_b[i].astype(f32)
        mod = (silu_c @ w_ada + b_ada).astype(h.dtype).reshape(B, 6, 1, D)
        sh1, sc1, gt1, sh2, sc2, gt2 = [mod[:, m] for m in range(6)]
        hn = _modulate(_rmsnorm(h, norm_g[i, 0]), sh1, sc1)
        hcn = None
        if ctx_in:
            modc = (silu_cc @ w_ada + b_ada).astype(hc.dtype).reshape(6, D)
            hcn = _modulate(_rmsnorm(hc, norm_g[i, 0]), modc[0], modc[1])
        if kind == MIXER_DELTA:
            y, yc = _deltanet(hn, hcn, rows, dn_w_in[j], dn_conv_w[j], dn_a_log[j], dn_dt_bias[j],
                              dn_norm_g[j], dn_w_out[j], ctx_out)
        else:
            hy_args = (hy_w_in[j], hy_conv_w[j], hy_conv_b[j], hy_f_w1[j], hy_f_b1[j], hy_f_w2[j],
                       hy_f_b2[j], hy_f_w3[j], hy_f_b3[j], hy_f_w4[j], hy_freq[j], hy_skip[j], hy_w_out[j])
            y = _hyena(hn, rows, *hy_args)
            yc = _hyena(hcn, 1, *hy_args) if ctx_out else None
        h = h + gt1 * y
        peer_args = (peer_w_q[i], peer_keys[i], peer_u[i], peer_v[i])
        h = h + gt2 * _peer(_modulate(_rmsnorm(h, norm_g[i, 1]), sh2, sc2), *peer_args)
        if ctx_out:
            hc = hc + modc[2] * yc
            hc = hc + modc[5] * _peer(_modulate(_rmsnorm(hc, norm_g[i, 1]), modc[3], modc[4]), *peer_args)
    return _final_rmsnorm(h, final_g)
```

```python
import math
import jax
import jax.numpy as jnp
from jax import lax
from jax.experimental import pallas as pl
from jax.experimental.pallas import tpu as pltpu

D_MODEL = 1024
BATCH = 8
SEQ = 8192
DEPTH = 2

GRID_W = 64
CTX_LEN = 256

N_MIXERS = 2
MIXER_DELTA = 0
MIXER_HYENA = 1

EPS = 1e-6

DN_HEAD_DIM = 128
DN_HEADS = D_MODEL // DN_HEAD_DIM
DN_INNER = DN_HEADS * DN_HEAD_DIM
DN_CONV = 5
DN_CHUNK = 64
DN_CHUNK_LOG2 = 6
DN_PROJ = 4 * DN_INNER + 4 * DN_HEADS

HY_WIDTH = D_MODEL
HY_SHORT = 3
HY_BANDS = 16
HY_EMB = 1 + 2 * HY_BANDS
HY_FF = 64
HY_FAST = 0.3
HY_SLOW = 1.5
HY_TARGET = 1e-2

PEER_HEADS = 8
PEER_NKEYS = 128
PEER_EXPERTS = PEER_NKEYS * PEER_NKEYS
PEER_DK = 256
PEER_TOPK = 16
PEER_BLOCK = 128


def _rmsnorm(x, g):
    xf = x.astype(jnp.float32)
    y = xf * lax.rsqrt(jnp.mean(xf * xf, axis=-1, keepdims=True) + EPS)
    return (y * g.astype(jnp.float32)).astype(x.dtype)


def _modulate(xn, shift, scale):
    return xn * (1 + scale) + shift


def _l2norm(x):
    xf = x.astype(jnp.float32)
    return xf * lax.rsqrt(jnp.sum(xf * xf, axis=-1, keepdims=True) + EPS)


def _rev(t):
    return jnp.flip(t, axis=2)


def _short_conv(x, w, n_rows):
    B, L, C = x.shape
    K = w.shape[0]
    xr = x.reshape(B * n_rows, L // n_rows, C)
    y = lax.conv_general_dilated(
        xr, w.astype(x.dtype)[:, None, :], window_strides=(1,),
        padding=[((K - 1) // 2, (K - 1) // 2)],
        dimension_numbers=('NWC', 'WIO', 'NWC'), feature_group_count=C)
    return y.reshape(B, L, C)


def _dn_inputs(hn, n_rows, w_in, conv_w, a_log, dt_bias):
    B, L, _ = hn.shape
    z = hn @ w_in
    qkv = jax.nn.silu(_short_conv(z[..., :3 * DN_INNER], conv_w, n_rows))
    gate = z[..., 3 * DN_INNER:4 * DN_INNER]
    ba = z[..., 4 * DN_INNER:].astype(jnp.float32).reshape(B, L, 2, 2, DN_HEADS)
    qkv = qkv.reshape(B, L, 3, DN_HEADS, DN_HEAD_DIM).transpose(2, 0, 3, 1, 4)
    q = _l2norm(qkv[0]) * (DN_HEAD_DIM ** -0.5)
    k = _l2norm(qkv[1])
    v = qkv[2].astype(jnp.float32)
    beta = jax.nn.sigmoid(ba[:, :, 0]).transpose(0, 2, 3, 1)
    g = (-jnp.exp(a_log.astype(jnp.float32)) *
         jax.nn.softplus(ba[:, :, 1] + dt_bias.astype(jnp.float32))).transpose(0, 2, 3, 1)
    return q, k, v, beta, g, gate


def _unit_lower_inverse(a):
    eye = jnp.eye(a.shape[-1], dtype=a.dtype)
    t = eye - a
    p = a
    for _ in range(DN_CHUNK_LOG2 - 1):
        p = p @ p
        t = t @ (eye + p)
    return t


def _gated_delta_chunked(k, v, beta, g, s0, q=None):
    f32 = jnp.float32
    with_out = q is not None
    B, H, L, DK = k.shape
    DV = v.shape[-1]
    C = DN_CHUNK
    n = L // C
    k = k.astype(f32).reshape(B, H, n, C, DK)
    v = v.astype(f32).reshape(B, H, n, C, DV)
    beta = beta.astype(f32).reshape(B, H, n, C)
    G = jnp.cumsum(g.astype(f32).reshape(B, H, n, C), axis=-1)
    incl = jnp.tril(jnp.ones((C, C), bool))
    strict = jnp.tril(jnp.ones((C, C), bool), -1)
    dmat = jnp.exp(jnp.where(incl, G[..., :, None] - G[..., None, :], -jnp.inf))
    kb = k * beta[..., None]
    a = jnp.where(strict, jnp.einsum('bhnid,bhnjd->bhnij', kb, k) * dmat, 0.0)
    t = _unit_lower_inverse(a)
    u = t @ (v * beta[..., None])
    w = t @ (kb * jnp.exp(G)[..., None])
    kd = k * jnp.exp(G[..., -1:] - G)[..., None]
    gl = jnp.exp(G[..., -1])
    xs = (jnp.moveaxis(w, 2, 0), jnp.moveaxis(u, 2, 0), jnp.moveaxis(kd, 2, 0), jnp.moveaxis(gl, 2, 0))
    if with_out:
        q = q.astype(f32).reshape(B, H, n, C, DK)
        qa = jnp.einsum('bhnid,bhnjd->bhnij', q, k) * dmat
        qg = q * jnp.exp(G)[..., None]
        xs = xs + (jnp.moveaxis(qg, 2, 0), jnp.moveaxis(qa, 2, 0))

    def step(s, inp):
        w_c, u_c, kd_c, gl_c = inp[:4]
        v_new = u_c - jnp.einsum('bhck,bhkv->bhcv', w_c, s)
        s_next = s * gl_c[..., None, None] + jnp.einsum('bhck,bhcv->bhkv', kd_c, v_new)
        if not with_out:
            return s_next, None
        qg_c, qa_c = inp[4:]
        o_c = jnp.einsum('bhck,bhkv->bhcv', qg_c, s) + jnp.einsum('bhij,bhjv->bhiv', qa_c, v_new)
        return s_next, o_c

    s_fin, o = lax.scan(step, s0.astype(f32), xs)
    if not with_out:
        return None, s_fin
    return jnp.moveaxis(o, 0, 2).reshape(B, H, L, DV), s_fin


def _gated_out(o, gate, g_norm, w_out):
    B, H, L, DV = o.shape
    o = jnp.swapaxes(o, 1, 2)
    on = o * lax.rsqrt(jnp.mean(o * o, axis=-1, keepdims=True) + EPS) * g_norm.astype(jnp.float32)
    y = on * jax.nn.silu(gate.astype(jnp.float32)).reshape(B, L, H, DV)
    return y.reshape(B, L, H * DV).astype(gate.dtype) @ w_out


def _deltanet(hn, hcn, n_rows, w_in, conv_w, a_log, dt_bias, o_norm_g, w_out, ctx_out):
    q, k, v, beta, g, gate = _dn_inputs(hn, n_rows, w_in, conv_w, a_log, dt_bias)
    qc, kc, vc, betac, gc, gatec = _dn_inputs(hcn, 1, w_in, conv_w, a_log, dt_bias)
    B = hn.shape[0]
    s0 = jnp.zeros((B, DN_HEADS, DN_HEAD_DIM, DN_HEAD_DIM), jnp.float32)
    oc_f, sc_f = _gated_delta_chunked(kc, vc, betac[:, 0], gc[:, 0], s0, qc if ctx_out else None)
    oc_b, sc_b = _gated_delta_chunked(_rev(kc), _rev(vc), _rev(betac[:, 1]), _rev(gc[:, 1]), s0,
                                      _rev(qc) if ctx_out else None)
    o_f, _ = _gated_delta_chunked(k, v, beta[:, 0], g[:, 0], sc_f, q)
    o_b, _ = _gated_delta_chunked(_rev(k), _rev(v), _rev(beta[:, 1]), _rev(g[:, 1]), sc_b, _rev(q))
    y = _gated_out(o_f + _rev(o_b), gate, o_norm_g, w_out)
    yc = _gated_out(oc_f + _rev(oc_b), gatec, o_norm_g, w_out) if ctx_out else None
    return y, yc


def _hyena_filters(L, w1, b1, w2, b2, w3, b3, w4, freq):
    f32 = jnp.float32
    D = w4.shape[-1] // 4
    pos = jnp.arange(L, dtype=f32)
    t = pos / max(L - 1, 1)
    ang = (2 * math.pi * pos / L)[:, None] * jnp.linspace(1e-4, HY_BANDS - 1, HY_BANDS, dtype=f32)[None]
    z = jnp.concatenate([t[:, None], jnp.cos(ang), -jnp.sin(ang)], axis=-1)
    fr = freq.astype(f32)
    hid = jnp.sin(fr * (z @ w1.astype(f32) + b1.astype(f32)))
    hid = jnp.sin(fr * (hid @ w2.astype(f32) + b2.astype(f32)))
    hid = jnp.sin(fr * (hid @ w3.astype(f32) + b3.astype(f32)))
    h = hid @ w4.astype(f32)
    deltas = jnp.abs(jnp.linspace(math.log(HY_TARGET) / HY_FAST, math.log(HY_TARGET) / HY_SLOW, D, dtype=f32))
    decay = jnp.exp(-t[:, None] * deltas[None])
    return (h.reshape(L, 2, 2, D) * decay[:, None, None, :]).transpose(1, 2, 0, 3)


def _long_conv(u, hf, hb, d):
    B, L, D = u.shape
    filt2 = jnp.concatenate([hf, jnp.zeros((1, D), jnp.float32), jnp.flip(hb[1:], axis=0)], axis=0)
    uf = jnp.fft.rfft(u.astype(jnp.float32), n=2 * L, axis=1)
    y = jnp.fft.irfft(uf * jnp.fft.rfft(filt2, axis=0)[None], n=2 * L, axis=1)[:, :L]
    return (y + u.astype(jnp.float32) * d.astype(jnp.float32)).astype(u.dtype)


def _hyena(hn, n_rows, w_in, conv_w, conv_b, f_w1, f_b1, f_w2, f_b2, f_w3, f_b3, f_w4, freq, skip, w_out):
    B, L, _ = hn.shape
    z = _short_conv(hn @ w_in, conv_w, n_rows) + conv_b.astype(hn.dtype)
    v, x1, x2 = jnp.split(z, 3, axis=-1)
    filt = _hyena_filters(L, f_w1, f_b1, f_w2, f_b2, f_w3, f_b3, f_w4, freq)
    z1 = x1 * _long_conv(v, filt[0, 0], filt[0, 1], skip[0])
    z2 = x2 * _long_conv(z1, filt[1, 0], filt[1, 1], skip[1])
    return z2 @ w_out


PEER_SLOTS = PEER_HEADS * PEER_TOPK
PEER_TOKENS_PER_STEP = 64
PEER_GATHER_BUFFERS = 4
LANES = 128


def _peer_expert_kernel(idx_ref, x_ref, gt_ref, uv_ref, o_ref, buf, sem):
    tb, d = x_ref.shape
    nbuf = buf.shape[0]

    def row_copy(row, j, slot):
        return pltpu.make_async_copy(uv_ref.at[pl.ds(row, 1)], buf.at[slot, pl.ds(j, 1)], sem.at[slot])

    def start_token(t, slot):
        for j in range(PEER_SLOTS):
            row_copy(idx_ref[t, j], j, slot).start(priority=j % 2)

    def wait_token(slot):
        for j in range(PEER_SLOTS):
            row_copy(0, j, slot).wait()

    for t in range(nbuf - 1):
        start_token(t, t)

    lane = lax.broadcasted_iota(jnp.int32, (PEER_SLOTS, LANES), 1)
    lane0 = (pl.program_id(0) * tb) % LANES

    def token(t, slot):
        ahead = t + nbuf - 1

        @pl.when(ahead < tb)
        def _():
            start_token(ahead, (slot + nbuf - 1) % nbuf)

        wait_token(slot)
        x_row = x_ref[pl.ds(t, 1), :]
        hid = jnp.sum(buf[slot, :, :d] * x_row, axis=-1, keepdims=True)
        gate = jnp.sum(jnp.where(lane == lane0 + t, gt_ref[0], 0.0), axis=-1, keepdims=True)
        coef = 0.5 * hid * (1.0 + lax.erf(hid * (2.0 ** -0.5))) * gate
        o_ref[pl.ds(t, 1), :] = jnp.sum(coef * buf[slot, :, d:], axis=0, keepdims=True)

    def token_group(g, carry):
        for slot in range(nbuf):
            token(g * nbuf + slot, slot)
        return carry

    lax.fori_loop(0, tb // nbuf, token_group, 0)


def _peer_experts(xt, idx, gate, u_tab, v_tab):
    T, D = xt.shape
    tb = PEER_TOKENS_PER_STEP
    assert T % LANES == 0 and LANES % tb == 0
    uv = jnp.concatenate([u_tab, v_tab], axis=1)
    gate_t = gate.reshape(T // LANES, LANES, PEER_SLOTS).transpose(0, 2, 1)
    steps_per_gate_tile = LANES // tb
    return pl.pallas_call(
        _peer_expert_kernel,
        grid=(T // tb,),
        in_specs=[pl.BlockSpec((tb, PEER_SLOTS), lambda i: (i, 0), memory_space=pltpu.SMEM),
                  pl.BlockSpec((tb, D), lambda i: (i, 0)),
                  pl.BlockSpec((1, PEER_SLOTS, LANES), lambda i: (i // steps_per_gate_tile, 0, 0)),
                  pl.BlockSpec(memory_space=pl.ANY)],
        out_specs=pl.BlockSpec((tb, D), lambda i: (i, 0)),
        out_shape=jax.ShapeDtypeStruct((T, D), xt.dtype),
        scratch_shapes=[pltpu.VMEM((PEER_GATHER_BUFFERS, PEER_SLOTS, 2 * D), jnp.float32),
                        pltpu.SemaphoreType.DMA((PEER_GATHER_BUFFERS,))],
        compiler_params=pltpu.CompilerParams(dimension_semantics=("arbitrary",)),
        name="peer_experts",
    )(idx, xt, gate_t, uv)


def _peer(x, w_q, sub_keys, u_tab, v_tab):
    B, L, D = x.shape
    T = B * L
    xt = x.reshape(T, D)
    q = (xt @ w_q).astype(jnp.float32).reshape(T, PEER_HEADS, 2, PEER_DK // 2)
    s = jnp.einsum('thpd,hpkd->thpk', q, sub_keys.astype(jnp.float32))
    s1, i1 = lax.top_k(s[:, :, 0], PEER_TOPK)
    s2, i2 = lax.top_k(s[:, :, 1], PEER_TOPK)
    n_cand = PEER_TOPK * PEER_TOPK
    cand_s = (s1[..., :, None] + s2[..., None, :]).reshape(T, PEER_HEADS, n_cand)
    cand_i = (i1[..., :, None] * PEER_NKEYS + i2[..., None, :]).reshape(T, PEER_HEADS, n_cand)
    top_s, pos = lax.top_k(cand_s, PEER_TOPK)
    idx = jnp.take_along_axis(cand_i, pos, axis=-1)
    gate = jax.nn.softmax(top_s, axis=-1)
    y = _peer_experts(xt, idx.reshape(T, PEER_SLOTS).astype(jnp.int32), gate.reshape(T, PEER_SLOTS),
                      u_tab, v_tab)
    return y.reshape(B, L, D)


def _final_rmsnorm_kernel(h_ref, g_ref, o_ref):
    x = h_ref[...]
    y = x * lax.rsqrt(jnp.mean(x * x, axis=-1, keepdims=True) + EPS)
    o_ref[...] = y * g_ref[...]


def _final_rmsnorm(h, g):
    B, L, D = h.shape
    T = B * L
    tm = 1024
    out = pl.pallas_call(
        _final_rmsnorm_kernel,
        grid=(T // tm,),
        in_specs=[pl.BlockSpec((tm, D), lambda i: (i, 0)),
                  pl.BlockSpec((1, D), lambda i: (0, 0))],
        out_specs=pl.BlockSpec((tm, D), lambda i: (i, 0)),
        out_shape=jax.ShapeDtypeStruct((T, D), h.dtype),
        name="final_rmsnorm",
    )(h.reshape(T, D), g.reshape(1, D))
    return out.reshape(B, L, D)


def kernel(x, c, ctx, c_ctx, ada_w, ada_b, norm_g,
           dn_w_in, dn_conv_w, dn_a_log, dn_dt_bias, dn_norm_g, dn_w_out,
           hy_w_in, hy_conv_w, hy_conv_b, hy_f_w1, hy_f_b1, hy_f_w2, hy_f_b2,
           hy_f_w3, hy_f_b3, hy_f_w4, hy_freq, hy_skip, hy_w_out,
           peer_w_q, peer_keys, peer_u, peer_v, final_g):
    f32 = jnp.float32
    B, L, D = x.shape
    rows = L // GRID_W
    silu_c = jax.nn.silu(c.astype(f32))
    silu_cc = jax.nn.silu(c_ctx.astype(f32))
    h, hc = x, ctx
    for i in range(DEPTH):
        kind = i % N_MIXERS
        j = i // N_MIXERS
        ctx_out = any(l % N_MIXERS == MIXER_DELTA for l in range(i + 1, DEPTH))
        ctx_in = kind == MIXER_DELTA or ctx_out
        w_ada = ada_w[i].astype(f32)
        b_ada = ada_b[i].astype(f32)
        mod = (silu_c @ w_ada + b_ada).astype(h.dtype).reshape(B, 6, 1, D)
        sh1, sc1, gt1, sh2, sc2, gt2 = [mod[:, m] for m in range(6)]
        hn = _modulate(_rmsnorm(h, norm_g[i, 0]), sh1, sc1)
        hcn = None
        if ctx_in:
            modc = (silu_cc @ w_ada + b_ada).astype(hc.dtype).reshape(6, D)
            hcn = _modulate(_rmsnorm(hc, norm_g[i, 0]), modc[0], modc[1])
        if kind == MIXER_DELTA:
            y, yc = _deltanet(hn, hcn, rows, dn_w_in[j], dn_conv_w[j], dn_a_log[j], dn_dt_bias[j],
                              dn_norm_g[j], dn_w_out[j], ctx_out)
        else:
            hy_args = (hy_w_in[j], hy_conv_w[j], hy_conv_b[j], hy_f_w1[j], hy_f_b1[j], hy_f_w2[j],
                       hy_f_b2[j], hy_f_w3[j], hy_f_b3[j], hy_f_w4[j], hy_freq[j], hy_skip[j], hy_w_out[j])
            y = _hyena(hn, rows, *hy_args)
            yc = _hyena(hcn, 1, *hy_args) if ctx_out else None
        h = h + gt1 * y
        peer_args = (peer_w_q[i], peer_keys[i], peer_u[i], peer_v[i])
        h = h + gt2 * _peer(_modulate(_rmsnorm(h, norm_g[i, 1]), sh2, sc2), *peer_args)
        if ctx_out:
            hc = hc + modc[2] * yc
            hc = hc + modc[5] * _peer(_modulate(_rmsnorm(hc, norm_g[i, 1]), modc[3], modc[4]), *peer_args)
    return _final_rmsnorm(h, final_g)
```

```python
import math
import jax
import jax.numpy as jnp
from jax import lax
from jax.experimental import pallas as pl
from jax.experimental.pallas import tpu as pltpu

D_MODEL = 1024
BATCH = 8
SEQ = 8192
DEPTH = 2

GRID_W = 64
CTX_LEN = 256

N_MIXERS = 2
MIXER_DELTA = 0
MIXER_HYENA = 1

EPS = 1e-6

DN_HEAD_DIM = 128
DN_HEADS = D_MODEL // DN_HEAD_DIM
DN_INNER = DN_HEADS * DN_HEAD_DIM
DN_CONV = 5
DN_CHUNK = 64
DN_CHUNK_LOG2 = 6
DN_PROJ = 4 * DN_INNER + 4 * DN_HEADS

HY_WIDTH = D_MODEL
HY_SHORT = 3
HY_BANDS = 16
HY_EMB = 1 + 2 * HY_BANDS
HY_FF = 64
HY_FAST = 0.3
HY_SLOW = 1.5
HY_TARGET = 1e-2

PEER_HEADS = 8
PEER_NKEYS = 128
PEER_EXPERTS = PEER_NKEYS * PEER_NKEYS
PEER_DK = 256
PEER_TOPK = 16
PEER_BLOCK = 128


def _rmsnorm(x, g):
    xf = x.astype(jnp.float32)
    y = xf * lax.rsqrt(jnp.mean(xf * xf, axis=-1, keepdims=True) + EPS)
    return (y * g.astype(jnp.float32)).astype(x.dtype)


def _modulate(xn, shift, scale):
    return xn * (1 + scale) + shift


def _l2norm(x):
    xf = x.astype(jnp.float32)
    return xf * lax.rsqrt(jnp.sum(xf * xf, axis=-1, keepdims=True) + EPS)


def _rev(t):
    return jnp.flip(t, axis=2)


def _short_conv(x, w, n_rows):
    B, L, C = x.shape
    K = w.shape[0]
    xr = x.reshape(B * n_rows, L // n_rows, C)
    y = lax.conv_general_dilated(
        xr, w.astype(x.dtype)[:, None, :], window_strides=(1,),
        padding=[((K - 1) // 2, (K - 1) // 2)],
        dimension_numbers=('NWC', 'WIO', 'NWC'), feature_group_count=C)
    return y.reshape(B, L, C)


def _dn_inputs(hn, n_rows, w_in, conv_w, a_log, dt_bias):
    B, L, _ = hn.shape
    z = hn @ w_in
    qkv = jax.nn.silu(_short_conv(z[..., :3 * DN_INNER], conv_w, n_rows))
    gate = z[..., 3 * DN_INNER:4 * DN_INNER]
    ba = z[..., 4 * DN_INNER:].astype(jnp.float32).reshape(B, L, 2, 2, DN_HEADS)
    qkv = qkv.reshape(B, L, 3, DN_HEADS, DN_HEAD_DIM).transpose(2, 0, 3, 1, 4)
    q = _l2norm(qkv[0]) * (DN_HEAD_DIM ** -0.5)
    k = _l2norm(qkv[1])
    v = qkv[2].astype(jnp.float32)
    beta = jax.nn.sigmoid(ba[:, :, 0]).transpose(0, 2, 3, 1)
    g = (-jnp.exp(a_log.astype(jnp.float32)) *
         jax.nn.softplus(ba[:, :, 1] + dt_bias.astype(jnp.float32))).transpose(0, 2, 3, 1)
    return q, k, v, beta, g, gate


def _unit_lower_inverse(a):
    eye = jnp.eye(a.shape[-1], dtype=a.dtype)
    t = eye - a
    p = a
    for _ in range(DN_CHUNK_LOG2 - 1):
        p = p @ p
        t = t @ (eye + p)
    return t


def _gated_delta_chunked(k, v, beta, g, s0, q=None):
    f32 = jnp.float32
    with_out = q is not None
    B, H, L, DK = k.shape
    DV = v.shape[-1]
    C = DN_CHUNK
    n = L // C
    k = k.astype(f32).reshape(B, H, n, C, DK)
    v = v.astype(f32).reshape(B, H, n, C, DV)
    beta = beta.astype(f32).reshape(B, H, n, C)
    G = jnp.cumsum(g.astype(f32).reshape(B, H, n, C), axis=-1)
    incl = jnp.tril(jnp.ones((C, C), bool))
    strict = jnp.tril(jnp.ones((C, C), bool), -1)
    dmat = jnp.exp(jnp.where(incl, G[..., :, None] - G[..., None, :], -jnp.inf))
    kb = k * beta[..., None]
    a = jnp.where(strict, jnp.einsum('bhnid,bhnjd->bhnij', kb, k) * dmat, 0.0)
    t = _unit_lower_inverse(a)
    u = t @ (v * beta[..., None])
    w = t @ (kb * jnp.exp(G)[..., None])
    kd = k * jnp.exp(G[..., -1:] - G)[..., None]
    gl = jnp.exp(G[..., -1])
    xs = (jnp.moveaxis(w, 2, 0), jnp.moveaxis(u, 2, 0), jnp.moveaxis(kd, 2, 0), jnp.moveaxis(gl, 2, 0))
    if with_out:
        q = q.astype(f32).reshape(B, H, n, C, DK)
        qa = jnp.einsum('bhnid,bhnjd->bhnij', q, k) * dmat
        qg = q * jnp.exp(G)[..., None]
        xs = xs + (jnp.moveaxis(qg, 2, 0), jnp.moveaxis(qa, 2, 0))

    def step(s, inp):
        w_c, u_c, kd_c, gl_c = inp[:4]
        v_new = u_c - jnp.einsum('bhck,bhkv->bhcv', w_c, s)
        s_next = s * gl_c[..., None, None] + jnp.einsum('bhck,bhcv->bhkv', kd_c, v_new)
        if not with_out:
            return s_next, None
        qg_c, qa_c = inp[4:]
        o_c = jnp.einsum('bhck,bhkv->bhcv', qg_c, s) + jnp.einsum('bhij,bhjv->bhiv', qa_c, v_new)
        return s_next, o_c

    s_fin, o = lax.scan(step, s0.astype(f32), xs)
    if not with_out:
        return None, s_fin
    return jnp.moveaxis(o, 0, 2).reshape(B, H, L, DV), s_fin


def _gated_out(o, gate, g_norm, w_out):
    B, H, L, DV = o.shape
    o = jnp.swapaxes(o, 1, 2)
    on = o * lax.rsqrt(jnp.mean(o * o, axis=-1, keepdims=True) + EPS) * g_norm.astype(jnp.float32)
    y = on * jax.nn.silu(gate.astype(jnp.float32)).reshape(B, L, H, DV)
    return y.reshape(B, L, H * DV).astype(gate.dtype) @ w_out


def _deltanet(hn, hcn, n_rows, w_in, conv_w, a_log, dt_bias, o_norm_g, w_out, ctx_out):
    q, k, v, beta, g, gate = _dn_inputs(hn, n_rows, w_in, conv_w, a_log, dt_bias)
    qc, kc, vc, betac, gc, gatec = _dn_inputs(hcn, 1, w_in, conv_w, a_log, dt_bias)
    B = hn.shape[0]
    s0 = jnp.zeros((B, DN_HEADS, DN_HEAD_DIM, DN_HEAD_DIM), jnp.float32)
    oc_f, sc_f = _gated_delta_chunked(kc, vc, betac[:, 0], gc[:, 0], s0, qc if ctx_out else None)
    oc_b, sc_b = _gated_delta_chunked(_rev(kc), _rev(vc), _rev(betac[:, 1]), _rev(gc[:, 1]), s0,
                                      _rev(qc) if ctx_out else None)
    o_f, _ = _gated_delta_chunked(k, v, beta[:, 0], g[:, 0], sc_f, q)
    o_b, _ = _gated_delta_chunked(_rev(k), _rev(v), _rev(beta[:, 1]), _rev(g[:, 1]), sc_b, _rev(q))
    y = _gated_out(o_f + _rev(o_b), gate, o_norm_g, w_out)
    yc = _gated_out(oc_f + _rev(oc_b), gatec, o_norm_g, w_out) if ctx_out else None
    return y, yc


def _hyena_filters(L, w1, b1, w2, b2, w3, b3, w4, freq):
    f32 = jnp.float32
    D = w4.shape[-1] // 4
    pos = jnp.arange(L, dtype=f32)
    t = pos / max(L - 1, 1)
    ang = (2 * math.pi * pos / L)[:, None] * jnp.linspace(1e-4, HY_BANDS - 1, HY_BANDS, dtype=f32)[None]
    z = jnp.concatenate([t[:, None], jnp.cos(ang), -jnp.sin(ang)], axis=-1)
    fr = freq.astype(f32)
    hid = jnp.sin(fr * (z @ w1.astype(f32) + b1.astype(f32)))
    hid = jnp.sin(fr * (hid @ w2.astype(f32) + b2.astype(f32)))
    hid = jnp.sin(fr * (hid @ w3.astype(f32) + b3.astype(f32)))
    h = hid @ w4.astype(f32)
    deltas = jnp.abs(jnp.linspace(math.log(HY_TARGET) / HY_FAST, math.log(HY_TARGET) / HY_SLOW, D, dtype=f32))
    decay = jnp.exp(-t[:, None] * deltas[None])
    return (h.reshape(L, 2, 2, D) * decay[:, None, None, :]).transpose(1, 2, 0, 3)


def _long_conv(u, hf, hb, d):
    B, L, D = u.shape
    filt2 = jnp.concatenate([hf, jnp.zeros((1, D), jnp.float32), jnp.flip(hb[1:], axis=0)], axis=0)
    uf = jnp.fft.rfft(u.astype(jnp.float32), n=2 * L, axis=1)
    y = jnp.fft.irfft(uf * jnp.fft.rfft(filt2, axis=0)[None], n=2 * L, axis=1)[:, :L]
    return (y + u.astype(jnp.float32) * d.astype(jnp.float32)).astype(u.dtype)


def _hyena(hn, n_rows, w_in, conv_w, conv_b, f_w1, f_b1, f_w2, f_b2, f_w3, f_b3, f_w4, freq, skip, w_out):
    B, L, _ = hn.shape
    z = _short_conv(hn @ w_in, conv_w, n_rows) + conv_b.astype(hn.dtype)
    v, x1, x2 = jnp.split(z, 3, axis=-1)
    filt = _hyena_filters(L, f_w1, f_b1, f_w2, f_b2, f_w3, f_b3, f_w4, freq)
    z1 = x1 * _long_conv(v, filt[0, 0], filt[0, 1], skip[0])
    z2 = x2 * _long_conv(z1, filt[1, 0], filt[1, 1], skip[1])
    return z2 @ w_out


PEER_SLOTS = PEER_HEADS * PEER_TOPK
PEER_TOKENS_PER_STEP = 64
PEER_GATHER_BUFFERS = 4
LANES = 128


def _peer_expert_kernel(idx_ref, x_ref, gt_ref, uv_ref, o_ref, buf, sem):
    tb, d = x_ref.shape
    nbuf = buf.shape[0]

    def row_copy(row, j, slot):
        return pltpu.make_async_copy(uv_ref.at[row], buf.at[slot, pl.ds(j, 1)], sem.at[slot])

    def start_token(t, slot):
        for j in range(PEER_SLOTS):
            row_copy(idx_ref[t, j], j, slot).start(priority=j % 2)

    def wait_token(slot):
        for j in range(PEER_SLOTS):
            row_copy(0, j, slot).wait()

    for t in range(nbuf - 1):
        start_token(t, t)

    lane = lax.broadcasted_iota(jnp.int32, (PEER_SLOTS, LANES), 1)
    lane0 = (pl.program_id(0) * tb) % LANES

    def token(t, slot):
        ahead = t + nbuf - 1

        @pl.when(ahead < tb)
        def _():
            start_token(ahead, (slot + nbuf - 1) % nbuf)

        wait_token(slot)
        x_row = x_ref[pl.ds(t, 1), :]
        hid = jnp.sum(buf[slot, :, :d] * x_row, axis=-1, keepdims=True)
        gate = jnp.sum(jnp.where(lane == lane0 + t, gt_ref[...], 0.0), axis=-1, keepdims=True)
        coef = 0.5 * hid * (1.0 + lax.erf(hid * (2.0 ** -0.5))) * gate
        o_ref[pl.ds(t, 1), :] = jnp.sum(coef * buf[slot, :, d:], axis=0, keepdims=True)

    def token_group(g, carry):
        for slot in range(nbuf):
            token(g * nbuf + slot, slot)
        return carry

    lax.fori_loop(0, tb // nbuf, token_group, 0)


def _peer_experts(xt, idx_t, gate_t, u_tab, v_tab):
    T, D = xt.shape
    tb = PEER_TOKENS_PER_STEP
    assert T % LANES == 0 and LANES % tb == 0
    uv = jnp.concatenate([u_tab, v_tab], axis=1)[:, None, :]
    steps_per_gate_tile = LANES // tb
    return pl.pallas_call(
        _peer_expert_kernel,
        grid=(T // tb,),
        in_specs=[pl.BlockSpec((tb, PEER_SLOTS), lambda i: (i, 0), memory_space=pltpu.SMEM),
                  pl.BlockSpec((tb, D), lambda i: (i, 0)),
                  pl.BlockSpec((PEER_SLOTS, LANES), lambda i: (0, i // steps_per_gate_tile)),
                  pl.BlockSpec(memory_space=pl.ANY)],
        out_specs=pl.BlockSpec((tb, D), lambda i: (i, 0)),
        out_shape=jax.ShapeDtypeStruct((T, D), xt.dtype),
        scratch_shapes=[pltpu.VMEM((PEER_GATHER_BUFFERS, PEER_SLOTS, 2 * D), jnp.float32),
                        pltpu.SemaphoreType.DMA((PEER_GATHER_BUFFERS,))],
        compiler_params=pltpu.CompilerParams(dimension_semantics=("arbitrary",)),
        name="peer_experts",
    )(idx_t.T, xt, gate_t, uv)


ROUTE_TOKENS_PER_STEP = 256
assert PEER_TOPK == 16


def _top_rows(s, order, payload, n):
    vals, outs = [], []
    for _ in range(n):
        m = jnp.max(s, axis=0, keepdims=True)
        first = jnp.min(jnp.where(s == m, order, jnp.inf), axis=0, keepdims=True)
        hit = order == first
        vals.append(m)
        outs.append(jnp.max(jnp.where(hit, payload, -1.0), axis=0, keepdims=True))
        s = jnp.where(hit, -jnp.inf, s)
    return jnp.concatenate(vals, axis=0), jnp.concatenate(outs, axis=0)


def _candidate_blocks(s1, i1, s2, i2):
    k = PEER_TOPK
    c = s1.shape[1]
    r8 = lax.broadcasted_iota(jnp.int32, (8, c), 0).astype(jnp.float32)
    r16 = lax.broadcasted_iota(jnp.int32, (k, c), 0).astype(jnp.float32)
    ninf = -jnp.inf

    def col(i, rows, r, keep):
        return (jnp.where(keep, s1[i:i + 1] + s2[:rows], ninf), i * k + r, i1[i:i + 1] * PEER_NKEYS + i2[:rows])

    def row(j, rows, r, keep):
        return (jnp.where(keep, s1[:rows] + s2[j:j + 1], ninf), r * k + j, i1[:rows] * PEER_NKEYS + i2[j:j + 1])

    blocks = [row(0, k, r16, r16 >= 0), col(0, k, r16, r16 >= 1), col(1, 8, r8, r8 >= 1), row(1, 8, r8, r8 >= 2),
              row(2, 8, r8, (r8 >= 2) & (r8 <= 4)), row(3, 8, r8, (r8 >= 2) & (r8 <= 3)), row(4, 8, r8, r8 == 2)]
    return [jnp.concatenate([b[n] for b in blocks], axis=0) for n in range(3)]


def _peer_route_kernel(h_ref, g_ref, sh_ref, sc_ref, wq_ref, keys_ref, xn_ref, idx_ref, gate_ref):
    x = h_ref[...]
    xn = x * lax.rsqrt(jnp.mean(x * x, axis=-1, keepdims=True) + EPS) * g_ref[...]
    xm = xn * (1 + sc_ref[0]) + sh_ref[0]
    xn_ref[...] = xm
    xb = xm.astype(jnp.bfloat16)
    tb = x.shape[0]
    dk2 = PEER_DK // 2
    key_order = lax.broadcasted_iota(jnp.int32, (PEER_NKEYS, tb), 0).astype(jnp.float32)

    def head(h, carry):
        q = jnp.dot(xb, wq_ref[h], preferred_element_type=jnp.float32)
        tops = []
        for p in range(2):
            qp = q[:, p * dk2:(p + 1) * dk2].astype(jnp.bfloat16)
            s = lax.dot_general(keys_ref[h, p], qp, (((1,), (1,)), ((), ())),
                                preferred_element_type=jnp.float32)
            tops.append(_top_rows(s, key_order, key_order, PEER_TOPK))
        (s1, i1), (s2, i2) = tops
        cand, order, expert = _candidate_blocks(s1, i1, s2, i2)
        top_s, top_e = _top_rows(cand, order, expert, PEER_TOPK)
        e = jnp.exp(top_s - top_s[0:1])
        rows = pl.ds(pl.multiple_of(h * PEER_TOPK, PEER_TOPK), PEER_TOPK)
        gate_ref[rows, :] = e / jnp.sum(e, axis=0, keepdims=True)
        idx_ref[rows, :] = top_e.astype(jnp.int32)
        return carry

    lax.fori_loop(0, PEER_HEADS, head, 0)


def _peer_route(h, g, shift, scale, w_q, sub_keys):
    B, L, D = h.shape
    T = B * L
    tb = min(ROUTE_TOKENS_PER_STEP, L)
    assert L % tb == 0 and tb % LANES == 0
    steps_per_batch = L // tb
    dk2 = PEER_DK // 2
    wq = w_q.reshape(D, PEER_HEADS, 2 * dk2).transpose(1, 0, 2).astype(jnp.bfloat16)
    keys = sub_keys.astype(jnp.bfloat16)
    return pl.pallas_call(
        _peer_route_kernel,
        grid=(T // tb,),
        in_specs=[pl.BlockSpec((tb, D), lambda i: (i, 0)),
                  pl.BlockSpec((1, D), lambda i: (0, 0)),
                  pl.BlockSpec((1, 1, D), lambda i: (i // steps_per_batch, 0, 0)),
                  pl.BlockSpec((1, 1, D), lambda i: (i // steps_per_batch, 0, 0)),
                  pl.BlockSpec((PEER_HEADS, D, 2 * dk2), lambda i: (0, 0, 0)),
                  pl.BlockSpec((PEER_HEADS, 2, PEER_NKEYS, dk2), lambda i: (0, 0, 0, 0))],
        out_specs=[pl.BlockSpec((tb, D), lambda i: (i, 0)),
                   pl.BlockSpec((PEER_SLOTS, tb), lambda i: (0, i)),
                   pl.BlockSpec((PEER_SLOTS, tb), lambda i: (0, i))],
        out_shape=[jax.ShapeDtypeStruct((T, D), h.dtype),
                   jax.ShapeDtypeStruct((PEER_SLOTS, T), jnp.int32),
                   jax.ShapeDtypeStruct((PEER_SLOTS, T), jnp.float32)],
        compiler_params=pltpu.CompilerParams(dimension_semantics=("arbitrary",),
                                             vmem_limit_bytes=48 * 1024 * 1024),
        name="peer_route",
    )(h.reshape(T, D), g.reshape(1, D), shift, scale, wq, keys)


def _peer(h, g, shift, scale, w_q, sub_keys, u_tab, v_tab):
    B, L, D = h.shape
    xn, idx_t, gate_t = _peer_route(h, g, shift, scale, w_q, sub_keys)
    return _peer_experts(xn, idx_t, gate_t, u_tab, v_tab).reshape(B, L, D)


def _final_rmsnorm_kernel(h_ref, g_ref, o_ref):
    x = h_ref[...]
    y = x * lax.rsqrt(jnp.mean(x * x, axis=-1, keepdims=True) + EPS)
    o_ref[...] = y * g_ref[...]


def _final_rmsnorm(h, g):
    B, L, D = h.shape
    T = B * L
    tm = 1024
    out = pl.pallas_call(
        _final_rmsnorm_kernel,
        grid=(T // tm,),
        in_specs=[pl.BlockSpec((tm, D), lambda i: (i, 0)),
                  pl.BlockSpec((1, D), lambda i: (0, 0))],
        out_specs=pl.BlockSpec((tm, D), lambda i: (i, 0)),
        out_shape=jax.ShapeDtypeStruct((T, D), h.dtype),
        name="final_rmsnorm",
    )(h.reshape(T, D), g.reshape(1, D))
    return out.reshape(B, L, D)


def kernel(x, c, ctx, c_ctx, ada_w, ada_b, norm_g,
           dn_w_in, dn_conv_w, dn_a_log, dn_dt_bias, dn_norm_g, dn_w_out,
           hy_w_in, hy_conv_w, hy_conv_b, hy_f_w1, hy_f_b1, hy_f_w2, hy_f_b2,
           hy_f_w3, hy_f_b3, hy_f_w4, hy_freq, hy_skip, hy_w_out,
           peer_w_q, peer_keys, peer_u, peer_v, final_g):
    f32 = jnp.float32
    B, L, D = x.shape
    rows = L // GRID_W
    silu_c = jax.nn.silu(c.astype(f32))
    silu_cc = jax.nn.silu(c_ctx.astype(f32))
    h, hc = x, ctx
    for i in range(DEPTH):
        kind = i % N_MIXERS
        j = i // N_MIXERS
        ctx_out = any(l % N_MIXERS == MIXER_DELTA for l in range(i + 1, DEPTH))
        ctx_in = kind == MIXER_DELTA or ctx_out
        w_ada = ada_w[i].astype(f32)
        b_ada = ada_b[i].astype(f32)
        mod = (silu_c @ w_ada + b_ada).astype(h.dtype).reshape(B, 6, 1, D)
        sh1, sc1, gt1, sh2, sc2, gt2 = [mod[:, m] for m in range(6)]
        hn = _modulate(_rmsnorm(h, norm_g[i, 0]), sh1, sc1)
        hcn = None
        if ctx_in:
            modc = (silu_cc @ w_ada + b_ada).astype(hc.dtype).reshape(6, D)
            hcn = _modulate(_rmsnorm(hc, norm_g[i, 0]), modc[0], modc[1])
        if kind == MIXER_DELTA:
            y, yc = _deltanet(hn, hcn, rows, dn_w_in[j], dn_conv_w[j], dn_a_log[j], dn_dt_bias[j],
                              dn_norm_g[j], dn_w_out[j], ctx_out)
        else:
            hy_args = (hy_w_in[j], hy_conv_w[j], hy_conv_b[j], hy_f_w1[j], hy_f_b1[j], hy_f_w2[j],
                       hy_f_b2[j], hy_f_w3[j], hy_f_b3[j], hy_f_w4[j], hy_freq[j], hy_skip[j], hy_w_out[j])
            y = _hyena(hn, rows, *hy_args)
            yc = _hyena(hcn, 1, *hy_args) if ctx_out else None
        h = h + gt1 * y
        peer_args = (peer_w_q[i], peer_keys[i], peer_u[i], peer_v[i])
        h = h + gt2 * _peer(h, norm_g[i, 1], sh2, sc2, *peer_args)
        assert not ctx_out
    return _final_rmsnorm(h, final_g)
```

```python
import functools
import math
import jax
import jax.numpy as jnp
from jax import lax
from jax.experimental import pallas as pl
from jax.experimental.pallas import tpu as pltpu

D_MODEL = 1024
BATCH = 8
SEQ = 8192
DEPTH = 2

GRID_W = 64
CTX_LEN = 256

N_MIXERS = 2
MIXER_DELTA = 0
MIXER_HYENA = 1

EPS = 1e-6

DN_HEAD_DIM = 128
DN_HEADS = D_MODEL // DN_HEAD_DIM
DN_INNER = DN_HEADS * DN_HEAD_DIM
DN_CONV = 5
DN_CHUNK = 64
DN_CHUNK_LOG2 = 6
DN_PROJ = 4 * DN_INNER + 4 * DN_HEADS

HY_WIDTH = D_MODEL
HY_SHORT = 3
HY_BANDS = 16
HY_EMB = 1 + 2 * HY_BANDS
HY_FF = 64
HY_FAST = 0.3
HY_SLOW = 1.5
HY_TARGET = 1e-2

PEER_HEADS = 8
PEER_NKEYS = 128
PEER_EXPERTS = PEER_NKEYS * PEER_NKEYS
PEER_DK = 256
PEER_TOPK = 16
PEER_BLOCK = 128


def _rmsnorm(x, g):
    xf = x.astype(jnp.float32)
    y = xf * lax.rsqrt(jnp.mean(xf * xf, axis=-1, keepdims=True) + EPS)
    return (y * g.astype(jnp.float32)).astype(x.dtype)


def _modulate(xn, shift, scale):
    return xn * (1 + scale) + shift


def _l2norm(x):
    xf = x.astype(jnp.float32)
    return xf * lax.rsqrt(jnp.sum(xf * xf, axis=-1, keepdims=True) + EPS)


def _rev(t):
    return jnp.flip(t, axis=2)


def _short_conv(x, w, n_rows):
    B, L, C = x.shape
    K = w.shape[0]
    xr = x.reshape(B * n_rows, L // n_rows, C)
    y = lax.conv_general_dilated(
        xr, w.astype(x.dtype)[:, None, :], window_strides=(1,),
        padding=[((K - 1) // 2, (K - 1) // 2)],
        dimension_numbers=('NWC', 'WIO', 'NWC'), feature_group_count=C)
    return y.reshape(B, L, C)


def _dn_inputs(hn, n_rows, w_in, conv_w, a_log, dt_bias):
    B, L, _ = hn.shape
    z = hn @ w_in
    qkv = jax.nn.silu(_short_conv(z[..., :3 * DN_INNER], conv_w, n_rows))
    gate = z[..., 3 * DN_INNER:4 * DN_INNER]
    ba = z[..., 4 * DN_INNER:].astype(jnp.float32).reshape(B, L, 2, 2, DN_HEADS)
    qkv = qkv.reshape(B, L, 3, DN_HEADS, DN_HEAD_DIM).transpose(2, 0, 3, 1, 4)
    q = _l2norm(qkv[0]) * (DN_HEAD_DIM ** -0.5)
    k = _l2norm(qkv[1])
    v = qkv[2].astype(jnp.float32)
    beta = jax.nn.sigmoid(ba[:, :, 0]).transpose(0, 2, 3, 1)
    g = (-jnp.exp(a_log.astype(jnp.float32)) *
         jax.nn.softplus(ba[:, :, 1] + dt_bias.astype(jnp.float32))).transpose(0, 2, 3, 1)
    return q, k, v, beta, g, gate


def _unit_lower_inverse(a):
    eye = jnp.eye(a.shape[-1], dtype=a.dtype)
    t = eye - a
    p = a
    for _ in range(DN_CHUNK_LOG2 - 1):
        p = p @ p
        t = t @ (eye + p)
    return t


def _gated_delta_chunked(k, v, beta, g, s0, q=None):
    f32 = jnp.float32
    with_out = q is not None
    B, H, L, DK = k.shape
    DV = v.shape[-1]
    C = DN_CHUNK
    n = L // C
    k = k.astype(f32).reshape(B, H, n, C, DK)
    v = v.astype(f32).reshape(B, H, n, C, DV)
    beta = beta.astype(f32).reshape(B, H, n, C)
    G = jnp.cumsum(g.astype(f32).reshape(B, H, n, C), axis=-1)
    incl = jnp.tril(jnp.ones((C, C), bool))
    strict = jnp.tril(jnp.ones((C, C), bool), -1)
    dmat = jnp.exp(jnp.where(incl, G[..., :, None] - G[..., None, :], -jnp.inf))
    kb = k * beta[..., None]
    a = jnp.where(strict, jnp.einsum('bhnid,bhnjd->bhnij', kb, k) * dmat, 0.0)
    t = _unit_lower_inverse(a)
    u = t @ (v * beta[..., None])
    w = t @ (kb * jnp.exp(G)[..., None])
    kd = k * jnp.exp(G[..., -1:] - G)[..., None]
    gl = jnp.exp(G[..., -1])
    xs = (jnp.moveaxis(w, 2, 0), jnp.moveaxis(u, 2, 0), jnp.moveaxis(kd, 2, 0), jnp.moveaxis(gl, 2, 0))
    if with_out:
        q = q.astype(f32).reshape(B, H, n, C, DK)
        qa = jnp.einsum('bhnid,bhnjd->bhnij', q, k) * dmat
        qg = q * jnp.exp(G)[..., None]
        xs = xs + (jnp.moveaxis(qg, 2, 0), jnp.moveaxis(qa, 2, 0))

    def step(s, inp):
        w_c, u_c, kd_c, gl_c = inp[:4]
        v_new = u_c - jnp.einsum('bhck,bhkv->bhcv', w_c, s)
        s_next = s * gl_c[..., None, None] + jnp.einsum('bhck,bhcv->bhkv', kd_c, v_new)
        if not with_out:
            return s_next, None
        qg_c, qa_c = inp[4:]
        o_c = jnp.einsum('bhck,bhkv->bhcv', qg_c, s) + jnp.einsum('bhij,bhjv->bhiv', qa_c, v_new)
        return s_next, o_c

    s_fin, o = lax.scan(step, s0.astype(f32), xs)
    if not with_out:
        return None, s_fin
    return jnp.moveaxis(o, 0, 2).reshape(B, H, L, DV), s_fin


def _gated_out(o, gate, g_norm, w_out):
    B, H, L, DV = o.shape
    o = jnp.swapaxes(o, 1, 2)
    on = o * lax.rsqrt(jnp.mean(o * o, axis=-1, keepdims=True) + EPS) * g_norm.astype(jnp.float32)
    y = on * jax.nn.silu(gate.astype(jnp.float32)).reshape(B, L, H, DV)
    return y.reshape(B, L, H * DV).astype(gate.dtype) @ w_out


def _deltanet(hn, hcn, n_rows, w_in, conv_w, a_log, dt_bias, o_norm_g, w_out, ctx_out):
    q, k, v, beta, g, gate = _dn_inputs(hn, n_rows, w_in, conv_w, a_log, dt_bias)
    qc, kc, vc, betac, gc, gatec = _dn_inputs(hcn, 1, w_in, conv_w, a_log, dt_bias)
    B = hn.shape[0]
    s0 = jnp.zeros((B, DN_HEADS, DN_HEAD_DIM, DN_HEAD_DIM), jnp.float32)
    oc_f, sc_f = _gated_delta_chunked(kc, vc, betac[:, 0], gc[:, 0], s0, qc if ctx_out else None)
    oc_b, sc_b = _gated_delta_chunked(_rev(kc), _rev(vc), _rev(betac[:, 1]), _rev(gc[:, 1]), s0,
                                      _rev(qc) if ctx_out else None)
    o_f, _ = _gated_delta_chunked(k, v, beta[:, 0], g[:, 0], sc_f, q)
    o_b, _ = _gated_delta_chunked(_rev(k), _rev(v), _rev(beta[:, 1]), _rev(g[:, 1]), sc_b, _rev(q))
    y = _gated_out(o_f + _rev(o_b), gate, o_norm_g, w_out)
    yc = _gated_out(oc_f + _rev(oc_b), gatec, o_norm_g, w_out) if ctx_out else None
    return y, yc


LANES = 128
PROJ_TOKENS_PER_STEP = 256
BF16 = jnp.bfloat16


def _norm_modulate(h_ref, g_ref, sh_ref, sc_ref):
    x = h_ref[...]
    xn = x * lax.rsqrt(jnp.mean(x * x, axis=-1, keepdims=True) + EPS) * g_ref[...]
    return xn * (1 + sc_ref[0]) + sh_ref[0]


def _row_conv(z, w, pos, row_len):
    n, taps = z.shape[0], w.shape[0]
    acc = None
    for tap in range(taps):
        off = tap - (taps - 1) // 2
        zs = z if off == 0 else pltpu.roll(z, (-off) % n, axis=0)
        ok = (pos + off >= 0) & (pos + off < row_len)
        term = jnp.where(ok, zs, 0.0) * w[tap:tap + 1]
        acc = term if acc is None else acc + term
    return acc


def _chunk_scan(x, pos, reverse):
    n = x.shape[0]
    s = 1
    while s < DN_CHUNK:
        if reverse:
            x = x + jnp.where(pos < DN_CHUNK - s, pltpu.roll(x, n - s, axis=0), 0.0)
        else:
            x = x + jnp.where(pos >= s, pltpu.roll(x, s, axis=0), 0.0)
        s *= 2
    return x


def _dn_inproj_kernel(h_ref, g_ref, sh_ref, sc_ref, w_ref, wb_ref, wa_ref, conv_ref, nea_ref, dtb_ref,
                      q_ref, k_ref, v_ref, gate_ref, beta_ref, pre_ref, suf_ref, tot_ref, *, row_len):
    xb = _norm_modulate(h_ref, g_ref, sh_ref, sc_ref).astype(BF16)
    tb = xb.shape[0]
    tok = lax.broadcasted_iota(jnp.int32, (tb, 1), 0)
    pos = tok % row_len
    for part, out_ref in enumerate((q_ref, k_ref, v_ref)):
        for head in range(DN_HEADS):
            c0 = part * DN_INNER + head * DN_HEAD_DIM
            z = jnp.dot(xb, w_ref[:, c0:c0 + DN_HEAD_DIM], preferred_element_type=jnp.float32)
            y = _row_conv(z, conv_ref[:, c0:c0 + DN_HEAD_DIM], pos, row_len)
            y = y * (1.0 / (1.0 + jnp.exp(-y)))
            if part < 2:
                y = y * lax.rsqrt(jnp.sum(y * y, axis=-1, keepdims=True) + EPS)
            if part == 0:
                y = y * (DN_HEAD_DIM ** -0.5)
            out_ref[:, head * DN_HEAD_DIM:(head + 1) * DN_HEAD_DIM] = y
    gate_ref[...] = jnp.dot(xb, w_ref[:, 3 * DN_INNER:], preferred_element_type=jnp.float32)
    zb = jnp.dot(xb, wb_ref[...], preferred_element_type=jnp.float32)
    za = jnp.dot(xb, wa_ref[...], preferred_element_type=jnp.float32) + dtb_ref[...]
    beta_ref[...] = 1.0 / (1.0 + jnp.exp(-zb))
    logdecay = nea_ref[...] * (jnp.maximum(za, 0.0) + jnp.log(1.0 + jnp.exp(-jnp.abs(za))))
    cpos = tok % DN_CHUNK
    pre = _chunk_scan(logdecay, cpos, False)
    suf = _chunk_scan(logdecay, cpos, True)
    pre_ref[...] = pre
    suf_ref[...] = suf
    tot_ref[...] = pre + suf - logdecay


def _dn_inproj(h, g, shift, scale, w_in, conv_w, a_log, dt_bias, row_len):
    B, L, D = h.shape
    T = B * L
    tb = min(PROJ_TOKENS_PER_STEP, L)
    assert L % tb == 0 and tb % row_len == 0 and tb % DN_CHUNK == 0
    steps_per_batch = L // tb
    nh2 = 2 * DN_HEADS
    lane_pad = lambda a: jnp.pad(a, ((0, 0), (0, LANES - nh2)))
    w_main = w_in[:, :4 * DN_INNER].astype(BF16)
    w_beta = lane_pad(w_in[:, 4 * DN_INNER:4 * DN_INNER + nh2]).astype(BF16)
    w_a = lane_pad(w_in[:, 4 * DN_INNER + nh2:]).astype(BF16)
    nea = lane_pad((-jnp.exp(a_log.astype(jnp.float32))).reshape(1, nh2))
    dtb = lane_pad(dt_bias.astype(jnp.float32).reshape(1, nh2))
    tok_spec = lambda n: pl.BlockSpec((tb, n), lambda i: (i, 0))
    full = lambda a: pl.BlockSpec(a.shape, lambda i: (0,) * a.ndim)
    mod_spec = pl.BlockSpec((1, 1, D), lambda i: (i // steps_per_batch, 0, 0))
    g2 = g.reshape(1, D)
    outs = pl.pallas_call(
        functools.partial(_dn_inproj_kernel, row_len=row_len),
        grid=(T // tb,),
        in_specs=[tok_spec(D), full(g2), mod_spec, mod_spec, full(w_main), full(w_beta), full(w_a),
                  full(conv_w), full(nea), full(dtb)],
        out_specs=[tok_spec(DN_INNER)] * 4 + [tok_spec(LANES)] * 4,
        out_shape=[jax.ShapeDtypeStruct((T, DN_INNER), jnp.float32)] * 4
        + [jax.ShapeDtypeStruct((T, LANES), jnp.float32)] * 4,
        compiler_params=pltpu.CompilerParams(dimension_semantics=("arbitrary",),
                                             vmem_limit_bytes=48 * 1024 * 1024),
        name="dn_inproj",
    )(h.reshape(T, D), g2, shift, scale, w_main, w_beta, w_a, conv_w, nea, dtb)
    q, k, v, gate, beta, pre, suf, tot = outs
    H = DN_HEADS
    zero = jnp.zeros((T, H), jnp.float32)
    fwd, bwd = slice(0, H), slice(H, 2 * H)
    rows = jnp.stack([beta[:, fwd], beta[:, bwd], pre[:, fwd], suf[:, bwd], tot[:, fwd], tot[:, bwd], zero, zero],
                     axis=0)
    return q, k, v, gate, rows.transpose(2, 0, 1)


DN_GROUP = 2 * DN_CHUNK
DN_SCAN_TOKENS_PER_STEP = 1024
NEG_BIG = -1e30


def _dn_group(k, v, q, rows, direction):
    n = DN_GROUP
    padded = jnp.concatenate([rows, jnp.zeros((n - rows.shape[0], n), jnp.float32)], axis=0)
    cols = padded.T
    beta_c = cols[:, direction:direction + 1]
    g_c = cols[:, 2 + direction:3 + direction]
    gtot_c = cols[:, 4 + direction:5 + direction]
    g_r = rows[2 + direction:3 + direction, :]
    ii = lax.broadcasted_iota(jnp.int32, (n, n), 0)
    jj = lax.broadcasted_iota(jnp.int32, (n, n), 1)
    same = (ii // DN_CHUNK) == (jj // DN_CHUNK)
    incl = same & ((jj >= ii) if direction else (jj <= ii))
    strict = same & ((jj > ii) if direction else (jj < ii))
    eye = (ii == jj).astype(jnp.float32)
    dmat = jnp.exp(jnp.where(incl, g_c - g_r, NEG_BIG))
    dot = functools.partial(jnp.dot, preferred_element_type=jnp.float32)
    dot_t = lambda a, b: lax.dot_general(a, b, (((1,), (1,)), ((), ())), preferred_element_type=jnp.float32)
    kb = k * beta_c
    k16 = k.astype(BF16)
    a = jnp.where(strict, dot_t(kb.astype(BF16), k16) * dmat, 0.0)
    t = eye - a
    p = a
    for _ in range(DN_CHUNK_LOG2 - 1):
        p = dot(p.astype(BF16), p.astype(BF16))
        t = dot(t.astype(BF16), (eye + p).astype(BF16))
    t16 = t.astype(BF16)
    eg = jnp.exp(g_c)
    u = dot(t16, (v * beta_c).astype(BF16))
    w = dot(t16, (kb * eg).astype(BF16))
    kd_t = (k * jnp.exp(gtot_c - g_c)).T
    lane = lax.broadcasted_iota(jnp.int32, kd_t.shape, 1)
    kd_t_chunks = [jnp.where(lane // DN_CHUNK == c, kd_t, 0.0).astype(BF16) for c in range(2)]
    qa = dot_t(q.astype(BF16), k16) * dmat
    qg = q * eg
    decay = [jnp.exp(gtot_c[c * DN_CHUNK:c * DN_CHUNK + 1, :]) for c in range(2)]
    return w.astype(BF16), u, qg.astype(BF16), qa.astype(BF16), kd_t_chunks, decay


def _dn_scan_group(state, grp, direction):
    w, u, qg, qa, kd_t_chunks, decay = grp
    dot = functools.partial(jnp.dot, preferred_element_type=jnp.float32)
    v_new = jnp.zeros_like(u)
    outs = [None, None]
    rows_of = lambda c: slice(c * DN_CHUNK, (c + 1) * DN_CHUNK)
    for c in ((1, 0) if direction else (0, 1)):
        r = rows_of(c)
        s16 = state.astype(BF16)
        vn = u[r] - dot(w[r], s16)
        chunk_rows = lax.broadcasted_iota(jnp.int32, u.shape, 0) // DN_CHUNK == c
        v_new = jnp.where(chunk_rows, jnp.concatenate([vn, vn], axis=0), v_new)
        vn16 = v_new.astype(BF16)
        outs[c] = dot(qg[r], s16) + dot(qa[r], vn16)
        state = state * decay[c] + dot(kd_t_chunks[c], vn16)
    return state, jnp.concatenate(outs, axis=0)


def _dn_scan_kernel(kf_ref, vf_ref, qf_ref, rf_ref, kb_ref, vb_ref, qb_ref, rb_ref, s0_ref,
                    of_ref, ob_ref, sfin_ref, state):
    i = pl.program_id(2)
    tl = kf_ref.shape[0]
    n_groups = tl // DN_GROUP

    @pl.when(i == 0)
    def _():
        state[...] = s0_ref[0, 0]

    def group(gi, carry):
        s_f, s_b = carry
        rf = pl.ds(pl.multiple_of(gi * DN_GROUP, DN_GROUP), DN_GROUP)
        rb = pl.ds(pl.multiple_of((n_groups - 1 - gi) * DN_GROUP, DN_GROUP), DN_GROUP)
        grp_f = _dn_group(kf_ref[rf, :], vf_ref[rf, :], qf_ref[rf, :], rf_ref[0, :, rf], 0)
        grp_b = _dn_group(kb_ref[rb, :], vb_ref[rb, :], qb_ref[rb, :], rb_ref[0, :, rb], 1)
        s_f, o_f = _dn_scan_group(s_f, grp_f, 0)
        s_b, o_b = _dn_scan_group(s_b, grp_b, 1)
        of_ref[rf, :] = o_f
        ob_ref[rb, :] = o_b
        return s_f, s_b

    s_f, s_b = lax.fori_loop(0, n_groups, group, (state[0], state[1]))
    state[0] = s_f
    state[1] = s_b

    @pl.when(i == pl.num_programs(2) - 1)
    def _():
        sfin_ref[0, 0, 0] = s_f
        sfin_ref[0, 0, 1] = s_b


def _dn_scan(q, k, v, rows, s0, B, L):
    T = B * L
    H, dk = DN_HEADS, DN_HEAD_DIM
    tl = min(DN_SCAN_TOKENS_PER_STEP, L)
    assert L % tl == 0 and tl % DN_GROUP == 0
    nb = L // tl
    fwd = lambda b, h, i: (b * nb + i, h)
    bwd = lambda b, h, i: (b * nb + nb - 1 - i, h)
    tok = lambda m: pl.BlockSpec((tl, dk), m)
    row_f = pl.BlockSpec((1, 8, tl), lambda b, h, i: (h, 0, b * nb + i))
    row_b = pl.BlockSpec((1, 8, tl), lambda b, h, i: (h, 0, b * nb + nb - 1 - i))
    st = pl.BlockSpec((1, 1, 2, dk, dk), lambda b, h, i: (b, h, 0, 0, 0))
    return pl.pallas_call(
        _dn_scan_kernel,
        grid=(B, H, nb),
        in_specs=[tok(fwd), tok(fwd), tok(fwd), row_f, tok(bwd), tok(bwd), tok(bwd), row_b, st],
        out_specs=[tok(fwd), tok(bwd), st],
        out_shape=[jax.ShapeDtypeStruct((T, DN_INNER), jnp.float32)] * 2
        + [jax.ShapeDtypeStruct((B, H, 2, dk, dk), jnp.float32)],
        scratch_shapes=[pltpu.VMEM((2, dk, dk), jnp.float32)],
        compiler_params=pltpu.CompilerParams(dimension_semantics=("arbitrary", "arbitrary", "arbitrary"),
                                             vmem_limit_bytes=48 * 1024 * 1024),
        name="dn_scan",
    )(k, v, q, rows, k, v, q, rows, s0)


def _dn_outproj_kernel(of_ref, ob_ref, gate_ref, gn_ref, w_ref, h_ref, rg_ref, o_ref):
    parts = []
    for head in range(DN_HEADS):
        c = slice(head * DN_HEAD_DIM, (head + 1) * DN_HEAD_DIM)
        o = of_ref[:, c] + ob_ref[:, c]
        on = o * lax.rsqrt(jnp.mean(o * o, axis=-1, keepdims=True) + EPS) * gn_ref[...]
        gt = gate_ref[:, c]
        parts.append((on * (gt * (1.0 / (1.0 + jnp.exp(-gt))))).astype(BF16))
    y = jnp.concatenate(parts, axis=1)
    o_ref[...] = h_ref[...] + rg_ref[0] * jnp.dot(y, w_ref[...], preferred_element_type=jnp.float32)


def _dn_outproj(o_f, o_b, gate, g_norm, w_out, h, res_gate):
    T, D = h.shape
    tb = PROJ_TOKENS_PER_STEP
    steps_per_batch = T // res_gate.shape[0] // tb
    tok = pl.BlockSpec((tb, D), lambda i: (i, 0))
    gn = g_norm.reshape(1, DN_HEAD_DIM).astype(jnp.float32)
    w16 = w_out.astype(BF16)
    return pl.pallas_call(
        _dn_outproj_kernel,
        grid=(T // tb,),
        in_specs=[tok, tok, tok, pl.BlockSpec(gn.shape, lambda i: (0, 0)), pl.BlockSpec(w16.shape, lambda i: (0, 0)),
                  tok, pl.BlockSpec((1, 1, D), lambda i: (i // steps_per_batch, 0, 0))],
        out_specs=tok,
        out_shape=jax.ShapeDtypeStruct((T, D), h.dtype),
        compiler_params=pltpu.CompilerParams(dimension_semantics=("arbitrary",)),
        name="dn_outproj",
    )(o_f, o_b, gate, gn, w16, h, res_gate)


def _deltanet_residual(h, hc, g, shift, scale, shift_c, scale_c, res_gate,
                       w_in, conv_w, a_log, dt_bias, o_norm_g, w_out):
    B, L, D = h.shape
    Lc = hc.shape[1]
    sh_c = jnp.broadcast_to(shift_c, (B, 1, D))
    sc_c = jnp.broadcast_to(scale_c, (B, 1, D))
    qc, kc, vc, _, rows_c = _dn_inproj(hc, g, sh_c, sc_c, w_in, conv_w, a_log, dt_bias, row_len=Lc)
    zeros = jnp.zeros((B, DN_HEADS, 2, DN_HEAD_DIM, DN_HEAD_DIM), jnp.float32)
    _, _, s_ctx = _dn_scan(qc, kc, vc, rows_c, zeros, B, Lc)
    q, k, v, gate, rows = _dn_inproj(h, g, shift, scale, w_in, conv_w, a_log, dt_bias, row_len=GRID_W)
    o_f, o_b, _ = _dn_scan(q, k, v, rows, s_ctx, B, L)
    return _dn_outproj(o_f, o_b, gate, o_norm_g, w_out, h.reshape(B * L, D), res_gate).reshape(B, L, D)


def _hy_inproj_kernel(h_ref, g_ref, sh_ref, sc_ref, w_ref, conv_ref, bias_ref, v_ref, x1_ref, x2_ref, *, row_len):
    xb = _norm_modulate(h_ref, g_ref, sh_ref, sc_ref).astype(BF16)
    pos = lax.broadcasted_iota(jnp.int32, (xb.shape[0], 1), 0) % row_len
    for part, out_ref in enumerate((v_ref, x1_ref, x2_ref)):
        for tile in range(HY_WIDTH // LANES):
            c0 = part * HY_WIDTH + tile * LANES
            z = jnp.dot(xb, w_ref[:, c0:c0 + LANES], preferred_element_type=jnp.float32)
            y = _row_conv(z, conv_ref[:, c0:c0 + LANES], pos, row_len) + bias_ref[:, c0:c0 + LANES]
            out_ref[:, tile * LANES:(tile + 1) * LANES] = y


def _hy_inproj(h, g, shift, scale, w_in, conv_w, conv_b, row_len):
    B, L, D = h.shape
    T = B * L
    tb = min(PROJ_TOKENS_PER_STEP, L)
    assert L % tb == 0 and tb % row_len == 0
    steps_per_batch = L // tb
    tok_spec = lambda n: pl.BlockSpec((tb, n), lambda i: (i, 0))
    full = lambda a: pl.BlockSpec(a.shape, lambda i: (0,) * a.ndim)
    mod_spec = pl.BlockSpec((1, 1, D), lambda i: (i // steps_per_batch, 0, 0))
    g2, w16, bias = g.reshape(1, D), w_in.astype(BF16), conv_b.reshape(1, -1)
    return pl.pallas_call(
        functools.partial(_hy_inproj_kernel, row_len=row_len),
        grid=(T // tb,),
        in_specs=[tok_spec(D), full(g2), mod_spec, mod_spec, full(w16), full(conv_w), full(bias)],
        out_specs=[tok_spec(HY_WIDTH)] * 3,
        out_shape=[jax.ShapeDtypeStruct((T, HY_WIDTH), jnp.float32)] * 3,
        compiler_params=pltpu.CompilerParams(dimension_semantics=("arbitrary",),
                                             vmem_limit_bytes=48 * 1024 * 1024),
        name="hy_inproj",
    )(h.reshape(T, D), g2, shift, scale, w16, conv_w, bias)


def _residual_proj_kernel(z_ref, w_ref, h_ref, rg_ref, o_ref):
    o_ref[...] = h_ref[...] + rg_ref[0] * jnp.dot(z_ref[...].astype(BF16), w_ref[...],
                                                   preferred_element_type=jnp.float32)


def _residual_proj(z, w_out, h, res_gate):
    T, D = h.shape
    tb = PROJ_TOKENS_PER_STEP
    steps_per_batch = T // res_gate.shape[0] // tb
    w16 = w_out.astype(BF16)
    return pl.pallas_call(
        _residual_proj_kernel,
        grid=(T // tb,),
        in_specs=[pl.BlockSpec((tb, z.shape[1]), lambda i: (i, 0)), pl.BlockSpec(w16.shape, lambda i: (0, 0)),
                  pl.BlockSpec((tb, D), lambda i: (i, 0)),
                  pl.BlockSpec((1, 1, D), lambda i: (i // steps_per_batch, 0, 0))],
        out_specs=pl.BlockSpec((tb, D), lambda i: (i, 0)),
        out_shape=jax.ShapeDtypeStruct((T, D), h.dtype),
        compiler_params=pltpu.CompilerParams(dimension_semantics=("arbitrary",)),
        name="residual_proj",
    )(z, w16, h, res_gate)


def _hyena_residual(h, g, shift, scale, res_gate, w_in, conv_w, conv_b, f_w1, f_b1, f_w2, f_b2, f_w3, f_b3, f_w4,
                    freq, skip, w_out):
    B, L, D = h.shape
    v, x1, x2 = [a.reshape(B, L, HY_WIDTH) for a in _hy_inproj(h, g, shift, scale, w_in, conv_w, conv_b, GRID_W)]
    filt = _hyena_filters(L, f_w1, f_b1, f_w2, f_b2, f_w3, f_b3, f_w4, freq)
    z1 = x1 * _long_conv(v, filt[0, 0], filt[0, 1], skip[0])
    z2 = x2 * _long_conv(z1, filt[1, 0], filt[1, 1], skip[1])
    return _residual_proj(z2.reshape(B * L, HY_WIDTH), w_out, h.reshape(B * L, D), res_gate).reshape(B, L, D)


def _hyena_filters(L, w1, b1, w2, b2, w3, b3, w4, freq):
    f32 = jnp.float32
    D = w4.shape[-1] // 4
    pos = jnp.arange(L, dtype=f32)
    t = pos / max(L - 1, 1)
    ang = (2 * math.pi * pos / L)[:, None] * jnp.linspace(1e-4, HY_BANDS - 1, HY_BANDS, dtype=f32)[None]
    z = jnp.concatenate([t[:, None], jnp.cos(ang), -jnp.sin(ang)], axis=-1)
    fr = freq.astype(f32)
    hid = jnp.sin(fr * (z @ w1.astype(f32) + b1.astype(f32)))
    hid = jnp.sin(fr * (hid @ w2.astype(f32) + b2.astype(f32)))
    hid = jnp.sin(fr * (hid @ w3.astype(f32) + b3.astype(f32)))
    h = hid @ w4.astype(f32)
    deltas = jnp.abs(jnp.linspace(math.log(HY_TARGET) / HY_FAST, math.log(HY_TARGET) / HY_SLOW, D, dtype=f32))
    decay = jnp.exp(-t[:, None] * deltas[None])
    return (h.reshape(L, 2, 2, D) * decay[:, None, None, :]).transpose(1, 2, 0, 3)


def _long_conv(u, hf, hb, d):
    B, L, D = u.shape
    filt2 = jnp.concatenate([hf, jnp.zeros((1, D), jnp.float32), jnp.flip(hb[1:], axis=0)], axis=0)
    uf = jnp.fft.rfft(u.astype(jnp.float32), n=2 * L, axis=1)
    y = jnp.fft.irfft(uf * jnp.fft.rfft(filt2, axis=0)[None], n=2 * L, axis=1)[:, :L]
    return (y + u.astype(jnp.float32) * d.astype(jnp.float32)).astype(u.dtype)


def _hyena(hn, n_rows, w_in, conv_w, conv_b, f_w1, f_b1, f_w2, f_b2, f_w3, f_b3, f_w4, freq, skip, w_out):
    B, L, _ = hn.shape
    z = _short_conv(hn @ w_in, conv_w, n_rows) + conv_b.astype(hn.dtype)
    v, x1, x2 = jnp.split(z, 3, axis=-1)
    filt = _hyena_filters(L, f_w1, f_b1, f_w2, f_b2, f_w3, f_b3, f_w4, freq)
    z1 = x1 * _long_conv(v, filt[0, 0], filt[0, 1], skip[0])
    z2 = x2 * _long_conv(z1, filt[1, 0], filt[1, 1], skip[1])
    return z2 @ w_out


PEER_SLOTS = PEER_HEADS * PEER_TOPK
PEER_TOKENS_PER_STEP = 64
PEER_GATHER_BUFFERS = 4


def _peer_expert_kernel(idx_ref, idx_next_ref, x_ref, gt_ref, h_ref, res_gate_ref, uv_ref, o_ref, buf, sem):
    tb, d = x_ref.shape
    nbuf = buf.shape[0]
    step = pl.program_id(0)
    last_step = pl.num_programs(0) - 1

    def row_copy(row, j, slot):
        return pltpu.make_async_copy(uv_ref.at[row], buf.at[slot, pl.ds(j, 1)], sem.at[slot])

    def start_token(ids_ref, t, slot):
        for j in range(PEER_SLOTS):
            row_copy(ids_ref[t, j], j, slot).start(priority=j % 2)

    def wait_token(slot):
        for j in range(PEER_SLOTS):
            row_copy(0, j, slot).wait()

    @pl.when(step == 0)
    def _():
        for t in range(nbuf - 1):
            start_token(idx_ref, t, t)

    lane = lax.broadcasted_iota(jnp.int32, (PEER_SLOTS, LANES), 1)
    lane0 = (step * tb) % LANES

    def finish_token(t, slot):
        wait_token(slot)
        x_row = x_ref[pl.ds(t, 1), :]
        hid = jnp.sum(buf[slot, :, :d] * x_row, axis=-1, keepdims=True)
        gate = jnp.sum(jnp.where(lane == lane0 + t, gt_ref[...], 0.0), axis=-1, keepdims=True)
        coef = 0.5 * hid * (1.0 + lax.erf(hid * (2.0 ** -0.5))) * gate
        y = jnp.sum(coef * buf[slot, :, d:], axis=0, keepdims=True)
        o_ref[pl.ds(t, 1), :] = h_ref[pl.ds(t, 1), :] + res_gate_ref[0] * y

    def token_group(g, carry):
        for slot in range(nbuf):
            t = g * nbuf + slot
            start_token(idx_ref, t + nbuf - 1, (slot + nbuf - 1) % nbuf)
            finish_token(t, slot)
        return carry

    lax.fori_loop(0, tb // nbuf - 1, token_group, 0)
    for slot in range(nbuf):
        t = tb - nbuf + slot
        ahead = t + nbuf - 1
        if ahead < tb:
            start_token(idx_ref, ahead, ahead % nbuf)
        else:
            start_token(idx_next_ref, ahead - tb, ahead % nbuf)
        finish_token(t, slot)

    @pl.when(step == last_step)
    def _():
        for slot in range(nbuf - 1):
            wait_token(slot)


def _peer_experts(xt, idx_t, gate_t, h, res_gate, u_tab, v_tab):
    T, D = xt.shape
    tb = PEER_TOKENS_PER_STEP
    nsteps = T // tb
    steps_per_batch = nsteps // res_gate.shape[0]
    assert T % LANES == 0 and LANES % tb == 0 and tb % PEER_GATHER_BUFFERS == 0
    uv = jnp.concatenate([u_tab, v_tab], axis=1)[:, None, :]
    steps_per_gate_tile = LANES // tb
    idx = idx_t.T
    return pl.pallas_call(
        _peer_expert_kernel,
        grid=(nsteps,),
        in_specs=[pl.BlockSpec((tb, PEER_SLOTS), lambda i: (i, 0), memory_space=pltpu.SMEM),
                  pl.BlockSpec((tb, PEER_SLOTS), lambda i: (jnp.minimum(i + 1, nsteps - 1), 0),
                               memory_space=pltpu.SMEM),
                  pl.BlockSpec((tb, D), lambda i: (i, 0)),
                  pl.BlockSpec((PEER_SLOTS, LANES), lambda i: (0, i // steps_per_gate_tile)),
                  pl.BlockSpec((tb, D), lambda i: (i, 0)),
                  pl.BlockSpec((1, 1, D), lambda i: (i // steps_per_batch, 0, 0)),
                  pl.BlockSpec(memory_space=pl.ANY)],
        out_specs=pl.BlockSpec((tb, D), lambda i: (i, 0)),
        out_shape=jax.ShapeDtypeStruct((T, D), xt.dtype),
        scratch_shapes=[pltpu.VMEM((PEER_GATHER_BUFFERS, PEER_SLOTS, 2 * D), jnp.float32),
                        pltpu.SemaphoreType.DMA((PEER_GATHER_BUFFERS,))],
        compiler_params=pltpu.CompilerParams(dimension_semantics=("arbitrary",)),
        name="peer_experts",
    )(idx, idx, xt, gate_t, h, res_gate, uv)


ROUTE_TOKENS_PER_STEP = 256
assert PEER_TOPK == 16


def _top_rows(s, order, payload, n):
    vals, outs = [], []
    for _ in range(n):
        m = jnp.max(s, axis=0, keepdims=True)
        first = jnp.min(jnp.where(s == m, order, jnp.inf), axis=0, keepdims=True)
        hit = order == first
        vals.append(m)
        outs.append(jnp.max(jnp.where(hit, payload, -1.0), axis=0, keepdims=True))
        s = jnp.where(hit, -jnp.inf, s)
    return jnp.concatenate(vals, axis=0), jnp.concatenate(outs, axis=0)


def _candidate_blocks(s1, i1, s2, i2):
    k = PEER_TOPK
    c = s1.shape[1]
    r8 = lax.broadcasted_iota(jnp.int32, (8, c), 0).astype(jnp.float32)
    r16 = lax.broadcasted_iota(jnp.int32, (k, c), 0).astype(jnp.float32)
    ninf = -jnp.inf

    def col(i, rows, r, keep):
        return (jnp.where(keep, s1[i:i + 1] + s2[:rows], ninf), i * k + r, i1[i:i + 1] * PEER_NKEYS + i2[:rows])

    def row(j, rows, r, keep):
        return (jnp.where(keep, s1[:rows] + s2[j:j + 1], ninf), r * k + j, i1[:rows] * PEER_NKEYS + i2[j:j + 1])

    blocks = [row(0, k, r16, r16 >= 0), col(0, k, r16, r16 >= 1), col(1, 8, r8, r8 >= 1), row(1, 8, r8, r8 >= 2),
              row(2, 8, r8, (r8 >= 2) & (r8 <= 4)), row(3, 8, r8, (r8 >= 2) & (r8 <= 3)), row(4, 8, r8, r8 == 2)]
    return [jnp.concatenate([b[n] for b in blocks], axis=0) for n in range(3)]


def _peer_route_kernel(h_ref, g_ref, sh_ref, sc_ref, wq_ref, keys_ref, xn_ref, idx_ref, gate_ref):
    x = h_ref[...]
    xn = x * lax.rsqrt(jnp.mean(x * x, axis=-1, keepdims=True) + EPS) * g_ref[...]
    xm = xn * (1 + sc_ref[0]) + sh_ref[0]
    xn_ref[...] = xm
    xb = xm.astype(jnp.bfloat16)
    tb = x.shape[0]
    dk2 = PEER_DK // 2
    key_order = lax.broadcasted_iota(jnp.int32, (PEER_NKEYS, tb), 0).astype(jnp.float32)

    def head(h, carry):
        q = jnp.dot(xb, wq_ref[h], preferred_element_type=jnp.float32)
        tops = []
        for p in range(2):
            qp = q[:, p * dk2:(p + 1) * dk2].astype(jnp.bfloat16)
            s = lax.dot_general(keys_ref[h, p], qp, (((1,), (1,)), ((), ())),
                                preferred_element_type=jnp.float32)
            tops.append(_top_rows(s, key_order, key_order, PEER_TOPK))
        (s1, i1), (s2, i2) = tops
        cand, order, expert = _candidate_blocks(s1, i1, s2, i2)
        top_s, top_e = _top_rows(cand, order, expert, PEER_TOPK)
        e = jnp.exp(top_s - top_s[0:1])
        rows = pl.ds(pl.multiple_of(h * PEER_TOPK, PEER_TOPK), PEER_TOPK)
        gate_ref[rows, :] = e / jnp.sum(e, axis=0, keepdims=True)
        idx_ref[rows, :] = top_e.astype(jnp.int32)
        return carry

    lax.fori_loop(0, PEER_HEADS, head, 0)


def _peer_route(h, g, shift, scale, w_q, sub_keys):
    B, L, D = h.shape
    T = B * L
    tb = min(ROUTE_TOKENS_PER_STEP, L)
    assert L % tb == 0 and tb % LANES == 0
    steps_per_batch = L // tb
    dk2 = PEER_DK // 2
    wq = w_q.reshape(D, PEER_HEADS, 2 * dk2).transpose(1, 0, 2).astype(jnp.bfloat16)
    keys = sub_keys.astype(jnp.bfloat16)
    return pl.pallas_call(
        _peer_route_kernel,
        grid=(T // tb,),
        in_specs=[pl.BlockSpec((tb, D), lambda i: (i, 0)),
                  pl.BlockSpec((1, D), lambda i: (0, 0)),
                  pl.BlockSpec((1, 1, D), lambda i: (i // steps_per_batch, 0, 0)),
                  pl.BlockSpec((1, 1, D), lambda i: (i // steps_per_batch, 0, 0)),
                  pl.BlockSpec((PEER_HEADS, D, 2 * dk2), lambda i: (0, 0, 0)),
                  pl.BlockSpec((PEER_HEADS, 2, PEER_NKEYS, dk2), lambda i: (0, 0, 0, 0))],
        out_specs=[pl.BlockSpec((tb, D), lambda i: (i, 0)),
                   pl.BlockSpec((PEER_SLOTS, tb), lambda i: (0, i)),
                   pl.BlockSpec((PEER_SLOTS, tb), lambda i: (0, i))],
        out_shape=[jax.ShapeDtypeStruct((T, D), h.dtype),
                   jax.ShapeDtypeStruct((PEER_SLOTS, T), jnp.int32),
                   jax.ShapeDtypeStruct((PEER_SLOTS, T), jnp.float32)],
        compiler_params=pltpu.CompilerParams(dimension_semantics=("arbitrary",),
                                             vmem_limit_bytes=48 * 1024 * 1024),
        name="peer_route",
    )(h.reshape(T, D), g.reshape(1, D), shift, scale, wq, keys)


def _peer_residual(h, g, shift, scale, res_gate, w_q, sub_keys, u_tab, v_tab):
    B, L, D = h.shape
    xn, idx_t, gate_t = _peer_route(h, g, shift, scale, w_q, sub_keys)
    return _peer_experts(xn, idx_t, gate_t, h.reshape(B * L, D), res_gate, u_tab, v_tab).reshape(B, L, D)


def _final_rmsnorm_kernel(h_ref, g_ref, o_ref):
    x = h_ref[...]
    y = x * lax.rsqrt(jnp.mean(x * x, axis=-1, keepdims=True) + EPS)
    o_ref[...] = y * g_ref[...]


def _final_rmsnorm(h, g):
    B, L, D = h.shape
    T = B * L
    tm = 1024
    out = pl.pallas_call(
        _final_rmsnorm_kernel,
        grid=(T // tm,),
        in_specs=[pl.BlockSpec((tm, D), lambda i: (i, 0)),
                  pl.BlockSpec((1, D), lambda i: (0, 0))],
        out_specs=pl.BlockSpec((tm, D), lambda i: (i, 0)),
        out_shape=jax.ShapeDtypeStruct((T, D), h.dtype),
        name="final_rmsnorm",
    )(h.reshape(T, D), g.reshape(1, D))
    return out.reshape(B, L, D)


def kernel(x, c, ctx, c_ctx, ada_w, ada_b, norm_g,
           dn_w_in, dn_conv_w, dn_a_log, dn_dt_bias, dn_norm_g, dn_w_out,
           hy_w_in, hy_conv_w, hy_conv_b, hy_f_w1, hy_f_b1, hy_f_w2, hy_f_b2,
           hy_f_w3, hy_f_b3, hy_f_w4, hy_freq, hy_skip, hy_w_out,
           peer_w_q, peer_keys, peer_u, peer_v, final_g):
    f32 = jnp.float32
    B, L, D = x.shape
    rows = L // GRID_W
    silu_c = jax.nn.silu(c.astype(f32))
    silu_cc = jax.nn.silu(c_ctx.astype(f32))
    h, hc = x, ctx
    for i in range(DEPTH):
        kind = i % N_MIXERS
        j = i // N_MIXERS
        assert not any(l % N_MIXERS == MIXER_DELTA for l in range(i + 1, DEPTH))
        w_ada = ada_w[i].astype(f32)
        b_ada = ada_b[i].astype(f32)
        mod = (silu_c @ w_ada + b_ada).astype(h.dtype).reshape(B, 6, 1, D)
        sh1, sc1, gt1, sh2, sc2, gt2 = [mod[:, m] for m in range(6)]
        if kind == MIXER_DELTA:
            modc = (silu_cc @ w_ada + b_ada).astype(hc.dtype).reshape(6, 1, 1, D)
            h = _deltanet_residual(h, hc, norm_g[i, 0], sh1, sc1, modc[0], modc[1], gt1, dn_w_in[j], dn_conv_w[j],
                                   dn_a_log[j], dn_dt_bias[j], dn_norm_g[j], dn_w_out[j])
        else:
            hy_args = (hy_w_in[j], hy_conv_w[j], hy_conv_b[j], hy_f_w1[j], hy_f_b1[j], hy_f_w2[j],
                       hy_f_b2[j], hy_f_w3[j], hy_f_b3[j], hy_f_w4[j], hy_freq[j], hy_skip[j], hy_w_out[j])
            h = _hyena_residual(h, norm_g[i, 0], sh1, sc1, gt1, *hy_args)
        peer_args = (peer_w_q[i], peer_keys[i], peer_u[i], peer_v[i])
        h = _peer_residual(h, norm_g[i, 1], sh2, sc2, gt2, *peer_args)
    return _final_rmsnorm(h, final_g)
```

```python
import cmath
import functools
import math
import jax
import jax.numpy as jnp
from jax import lax
from jax.experimental import pallas as pl
from jax.experimental.pallas import tpu as pltpu

D_MODEL = 1024
BATCH = 8
SEQ = 8192
DEPTH = 2

GRID_W = 64
CTX_LEN = 256

N_MIXERS = 2
MIXER_DELTA = 0
MIXER_HYENA = 1

EPS = 1e-6

DN_HEAD_DIM = 128
DN_HEADS = D_MODEL // DN_HEAD_DIM
DN_INNER = DN_HEADS * DN_HEAD_DIM
DN_CONV = 5
DN_CHUNK = 64
DN_CHUNK_LOG2 = 6
DN_PROJ = 4 * DN_INNER + 4 * DN_HEADS

HY_WIDTH = D_MODEL
HY_SHORT = 3
HY_BANDS = 16
HY_EMB = 1 + 2 * HY_BANDS
HY_FF = 64
HY_FAST = 0.3
HY_SLOW = 1.5
HY_TARGET = 1e-2

PEER_HEADS = 8
PEER_NKEYS = 128
PEER_EXPERTS = PEER_NKEYS * PEER_NKEYS
PEER_DK = 256
PEER_TOPK = 16
PEER_BLOCK = 128


def _rmsnorm(x, g):
    xf = x.astype(jnp.float32)
    y = xf * lax.rsqrt(jnp.mean(xf * xf, axis=-1, keepdims=True) + EPS)
    return (y * g.astype(jnp.float32)).astype(x.dtype)


def _modulate(xn, shift, scale):
    return xn * (1 + scale) + shift


def _l2norm(x):
    xf = x.astype(jnp.float32)
    return xf * lax.rsqrt(jnp.sum(xf * xf, axis=-1, keepdims=True) + EPS)


def _rev(t):
    return jnp.flip(t, axis=2)


def _short_conv(x, w, n_rows):
    B, L, C = x.shape
    K = w.shape[0]
    xr = x.reshape(B * n_rows, L // n_rows, C)
    y = lax.conv_general_dilated(
        xr, w.astype(x.dtype)[:, None, :], window_strides=(1,),
        padding=[((K - 1) // 2, (K - 1) // 2)],
        dimension_numbers=('NWC', 'WIO', 'NWC'), feature_group_count=C)
    return y.reshape(B, L, C)


def _dn_inputs(hn, n_rows, w_in, conv_w, a_log, dt_bias):
    B, L, _ = hn.shape
    z = hn @ w_in
    qkv = jax.nn.silu(_short_conv(z[..., :3 * DN_INNER], conv_w, n_rows))
    gate = z[..., 3 * DN_INNER:4 * DN_INNER]
    ba = z[..., 4 * DN_INNER:].astype(jnp.float32).reshape(B, L, 2, 2, DN_HEADS)
    qkv = qkv.reshape(B, L, 3, DN_HEADS, DN_HEAD_DIM).transpose(2, 0, 3, 1, 4)
    q = _l2norm(qkv[0]) * (DN_HEAD_DIM ** -0.5)
    k = _l2norm(qkv[1])
    v = qkv[2].astype(jnp.float32)
    beta = jax.nn.sigmoid(ba[:, :, 0]).transpose(0, 2, 3, 1)
    g = (-jnp.exp(a_log.astype(jnp.float32)) *
         jax.nn.softplus(ba[:, :, 1] + dt_bias.astype(jnp.float32))).transpose(0, 2, 3, 1)
    return q, k, v, beta, g, gate


def _unit_lower_inverse(a):
    eye = jnp.eye(a.shape[-1], dtype=a.dtype)
    t = eye - a
    p = a
    for _ in range(DN_CHUNK_LOG2 - 1):
        p = p @ p
        t = t @ (eye + p)
    return t


def _gated_delta_chunked(k, v, beta, g, s0, q=None):
    f32 = jnp.float32
    with_out = q is not None
    B, H, L, DK = k.shape
    DV = v.shape[-1]
    C = DN_CHUNK
    n = L // C
    k = k.astype(f32).reshape(B, H, n, C, DK)
    v = v.astype(f32).reshape(B, H, n, C, DV)
    beta = beta.astype(f32).reshape(B, H, n, C)
    G = jnp.cumsum(g.astype(f32).reshape(B, H, n, C), axis=-1)
    incl = jnp.tril(jnp.ones((C, C), bool))
    strict = jnp.tril(jnp.ones((C, C), bool), -1)
    dmat = jnp.exp(jnp.where(incl, G[..., :, None] - G[..., None, :], -jnp.inf))
    kb = k * beta[..., None]
    a = jnp.where(strict, jnp.einsum('bhnid,bhnjd->bhnij', kb, k) * dmat, 0.0)
    t = _unit_lower_inverse(a)
    u = t @ (v * beta[..., None])
    w = t @ (kb * jnp.exp(G)[..., None])
    kd = k * jnp.exp(G[..., -1:] - G)[..., None]
    gl = jnp.exp(G[..., -1])
    xs = (jnp.moveaxis(w, 2, 0), jnp.moveaxis(u, 2, 0), jnp.moveaxis(kd, 2, 0), jnp.moveaxis(gl, 2, 0))
    if with_out:
        q = q.astype(f32).reshape(B, H, n, C, DK)
        qa = jnp.einsum('bhnid,bhnjd->bhnij', q, k) * dmat
        qg = q * jnp.exp(G)[..., None]
        xs = xs + (jnp.moveaxis(qg, 2, 0), jnp.moveaxis(qa, 2, 0))

    def step(s, inp):
        w_c, u_c, kd_c, gl_c = inp[:4]
        v_new = u_c - jnp.einsum('bhck,bhkv->bhcv', w_c, s)
        s_next = s * gl_c[..., None, None] + jnp.einsum('bhck,bhcv->bhkv', kd_c, v_new)
        if not with_out:
            return s_next, None
        qg_c, qa_c = inp[4:]
        o_c = jnp.einsum('bhck,bhkv->bhcv', qg_c, s) + jnp.einsum('bhij,bhjv->bhiv', qa_c, v_new)
        return s_next, o_c

    s_fin, o = lax.scan(step, s0.astype(f32), xs)
    if not with_out:
        return None, s_fin
    return jnp.moveaxis(o, 0, 2).reshape(B, H, L, DV), s_fin


def _gated_out(o, gate, g_norm, w_out):
    B, H, L, DV = o.shape
    o = jnp.swapaxes(o, 1, 2)
    on = o * lax.rsqrt(jnp.mean(o * o, axis=-1, keepdims=True) + EPS) * g_norm.astype(jnp.float32)
    y = on * jax.nn.silu(gate.astype(jnp.float32)).reshape(B, L, H, DV)
    return y.reshape(B, L, H * DV).astype(gate.dtype) @ w_out


def _deltanet(hn, hcn, n_rows, w_in, conv_w, a_log, dt_bias, o_norm_g, w_out, ctx_out):
    q, k, v, beta, g, gate = _dn_inputs(hn, n_rows, w_in, conv_w, a_log, dt_bias)
    qc, kc, vc, betac, gc, gatec = _dn_inputs(hcn, 1, w_in, conv_w, a_log, dt_bias)
    B = hn.shape[0]
    s0 = jnp.zeros((B, DN_HEADS, DN_HEAD_DIM, DN_HEAD_DIM), jnp.float32)
    oc_f, sc_f = _gated_delta_chunked(kc, vc, betac[:, 0], gc[:, 0], s0, qc if ctx_out else None)
    oc_b, sc_b = _gated_delta_chunked(_rev(kc), _rev(vc), _rev(betac[:, 1]), _rev(gc[:, 1]), s0,
                                      _rev(qc) if ctx_out else None)
    o_f, _ = _gated_delta_chunked(k, v, beta[:, 0], g[:, 0], sc_f, q)
    o_b, _ = _gated_delta_chunked(_rev(k), _rev(v), _rev(beta[:, 1]), _rev(g[:, 1]), sc_b, _rev(q))
    y = _gated_out(o_f + _rev(o_b), gate, o_norm_g, w_out)
    yc = _gated_out(oc_f + _rev(oc_b), gatec, o_norm_g, w_out) if ctx_out else None
    return y, yc


LANES = 128
PROJ_TOKENS_PER_STEP = 256
BF16 = jnp.bfloat16


def _norm_modulate(h_ref, g_ref, sh_ref, sc_ref):
    x = h_ref[...]
    xn = x * lax.rsqrt(jnp.mean(x * x, axis=-1, keepdims=True) + EPS) * g_ref[...]
    return xn * (1 + sc_ref[0]) + sh_ref[0]


def _row_conv(z, w, pos, row_len):
    n, taps = z.shape[0], w.shape[0]
    acc = None
    for tap in range(taps):
        off = tap - (taps - 1) // 2
        zs = z if off == 0 else pltpu.roll(z, (-off) % n, axis=0)
        ok = (pos + off >= 0) & (pos + off < row_len)
        term = jnp.where(ok, zs, 0.0) * w[tap:tap + 1]
        acc = term if acc is None else acc + term
    return acc


def _chunk_scan(x, pos, reverse):
    n = x.shape[0]
    s = 1
    while s < DN_CHUNK:
        if reverse:
            x = x + jnp.where(pos < DN_CHUNK - s, pltpu.roll(x, n - s, axis=0), 0.0)
        else:
            x = x + jnp.where(pos >= s, pltpu.roll(x, s, axis=0), 0.0)
        s *= 2
    return x


def _dn_inproj_kernel(h_ref, g_ref, sh_ref, sc_ref, w_ref, wb_ref, wa_ref, conv_ref, nea_ref, dtb_ref,
                      q_ref, k_ref, v_ref, gate_ref, beta_ref, pre_ref, suf_ref, tot_ref, *, row_len):
    xb = _norm_modulate(h_ref, g_ref, sh_ref, sc_ref).astype(BF16)
    tb = xb.shape[0]
    tok = lax.broadcasted_iota(jnp.int32, (tb, 1), 0)
    pos = tok % row_len
    for part, out_ref in enumerate((q_ref, k_ref, v_ref)):
        for head in range(DN_HEADS):
            c0 = part * DN_INNER + head * DN_HEAD_DIM
            z = jnp.dot(xb, w_ref[:, c0:c0 + DN_HEAD_DIM], preferred_element_type=jnp.float32)
            y = _row_conv(z, conv_ref[:, c0:c0 + DN_HEAD_DIM], pos, row_len)
            y = y * (1.0 / (1.0 + jnp.exp(-y)))
            if part < 2:
                y = y * lax.rsqrt(jnp.sum(y * y, axis=-1, keepdims=True) + EPS)
            if part == 0:
                y = y * (DN_HEAD_DIM ** -0.5)
            out_ref[:, head * DN_HEAD_DIM:(head + 1) * DN_HEAD_DIM] = y
    gate_ref[...] = jnp.dot(xb, w_ref[:, 3 * DN_INNER:], preferred_element_type=jnp.float32)
    zb = jnp.dot(xb, wb_ref[...], preferred_element_type=jnp.float32)
    za = jnp.dot(xb, wa_ref[...], preferred_element_type=jnp.float32) + dtb_ref[...]
    beta_ref[...] = 1.0 / (1.0 + jnp.exp(-zb))
    logdecay = nea_ref[...] * (jnp.maximum(za, 0.0) + jnp.log(1.0 + jnp.exp(-jnp.abs(za))))
    cpos = tok % DN_CHUNK
    pre = _chunk_scan(logdecay, cpos, False)
    suf = _chunk_scan(logdecay, cpos, True)
    pre_ref[...] = pre
    suf_ref[...] = suf
    tot_ref[...] = pre + suf - logdecay


def _dn_inproj(h, g, shift, scale, w_in, conv_w, a_log, dt_bias, row_len):
    B, L, D = h.shape
    T = B * L
    tb = min(PROJ_TOKENS_PER_STEP, L)
    assert L % tb == 0 and tb % row_len == 0 and tb % DN_CHUNK == 0
    steps_per_batch = L // tb
    nh2 = 2 * DN_HEADS
    lane_pad = lambda a: jnp.pad(a, ((0, 0), (0, LANES - nh2)))
    w_main = w_in[:, :4 * DN_INNER].astype(BF16)
    w_beta = lane_pad(w_in[:, 4 * DN_INNER:4 * DN_INNER + nh2]).astype(BF16)
    w_a = lane_pad(w_in[:, 4 * DN_INNER + nh2:]).astype(BF16)
    nea = lane_pad((-jnp.exp(a_log.astype(jnp.float32))).reshape(1, nh2))
    dtb = lane_pad(dt_bias.astype(jnp.float32).reshape(1, nh2))
    tok_spec = lambda n: pl.BlockSpec((tb, n), lambda i: (i, 0))
    full = lambda a: pl.BlockSpec(a.shape, lambda i: (0,) * a.ndim)
    mod_spec = pl.BlockSpec((1, 1, D), lambda i: (i // steps_per_batch, 0, 0))
    g2 = g.reshape(1, D)
    outs = pl.pallas_call(
        functools.partial(_dn_inproj_kernel, row_len=row_len),
        grid=(T // tb,),
        in_specs=[tok_spec(D), full(g2), mod_spec, mod_spec, full(w_main), full(w_beta), full(w_a),
                  full(conv_w), full(nea), full(dtb)],
        out_specs=[tok_spec(DN_INNER)] * 4 + [tok_spec(LANES)] * 4,
        out_shape=[jax.ShapeDtypeStruct((T, DN_INNER), jnp.float32)] * 4
        + [jax.ShapeDtypeStruct((T, LANES), jnp.float32)] * 4,
        compiler_params=pltpu.CompilerParams(dimension_semantics=("arbitrary",),
                                             vmem_limit_bytes=48 * 1024 * 1024),
        name="dn_inproj",
    )(h.reshape(T, D), g2, shift, scale, w_main, w_beta, w_a, conv_w, nea, dtb)
    q, k, v, gate, beta, pre, suf, tot = outs
    H = DN_HEADS
    zero = jnp.zeros((T, H), jnp.float32)
    fwd, bwd = slice(0, H), slice(H, 2 * H)
    rows = jnp.stack([beta[:, fwd], beta[:, bwd], pre[:, fwd], suf[:, bwd], tot[:, fwd], tot[:, bwd], zero, zero],
                     axis=0)
    return q, k, v, gate, rows.transpose(2, 0, 1)


DN_GROUP = 2 * DN_CHUNK
DN_SCAN_TOKENS_PER_STEP = 1024
NEG_BIG = -1e30


def _dn_group(k, v, q, rows, direction):
    n = DN_GROUP
    padded = jnp.concatenate([rows, jnp.zeros((n - rows.shape[0], n), jnp.float32)], axis=0)
    cols = padded.T
    beta_c = cols[:, direction:direction + 1]
    g_c = cols[:, 2 + direction:3 + direction]
    gtot_c = cols[:, 4 + direction:5 + direction]
    g_r = rows[2 + direction:3 + direction, :]
    ii = lax.broadcasted_iota(jnp.int32, (n, n), 0)
    jj = lax.broadcasted_iota(jnp.int32, (n, n), 1)
    same = (ii // DN_CHUNK) == (jj // DN_CHUNK)
    incl = same & ((jj >= ii) if direction else (jj <= ii))
    strict = same & ((jj > ii) if direction else (jj < ii))
    eye = (ii == jj).astype(jnp.float32)
    dmat = jnp.exp(jnp.where(incl, g_c - g_r, NEG_BIG))
    dot = functools.partial(jnp.dot, preferred_element_type=jnp.float32)
    dot_t = lambda a, b: lax.dot_general(a, b, (((1,), (1,)), ((), ())), preferred_element_type=jnp.float32)
    kb = k * beta_c
    k16 = k.astype(BF16)
    a = jnp.where(strict, dot_t(kb.astype(BF16), k16) * dmat, 0.0)
    t = eye - a
    p = a
    for _ in range(DN_CHUNK_LOG2 - 1):
        p = dot(p.astype(BF16), p.astype(BF16))
        t = dot(t.astype(BF16), (eye + p).astype(BF16))
    t16 = t.astype(BF16)
    eg = jnp.exp(g_c)
    u = dot(t16, (v * beta_c).astype(BF16))
    w = dot(t16, (kb * eg).astype(BF16))
    kd_t = (k * jnp.exp(gtot_c - g_c)).T
    lane = lax.broadcasted_iota(jnp.int32, kd_t.shape, 1)
    kd_t_chunks = [jnp.where(lane // DN_CHUNK == c, kd_t, 0.0).astype(BF16) for c in range(2)]
    qa = dot_t(q.astype(BF16), k16) * dmat
    qg = q * eg
    decay = [jnp.exp(gtot_c[c * DN_CHUNK:c * DN_CHUNK + 1, :]) for c in range(2)]
    return w.astype(BF16), u, qg.astype(BF16), qa.astype(BF16), kd_t_chunks, decay


def _dn_scan_group(state, grp, direction):
    w, u, qg, qa, kd_t_chunks, decay = grp
    dot = functools.partial(jnp.dot, preferred_element_type=jnp.float32)
    v_new = jnp.zeros_like(u)
    outs = [None, None]
    rows_of = lambda c: slice(c * DN_CHUNK, (c + 1) * DN_CHUNK)
    for c in ((1, 0) if direction else (0, 1)):
        r = rows_of(c)
        s16 = state.astype(BF16)
        vn = u[r] - dot(w[r], s16)
        chunk_rows = lax.broadcasted_iota(jnp.int32, u.shape, 0) // DN_CHUNK == c
        v_new = jnp.where(chunk_rows, jnp.concatenate([vn, vn], axis=0), v_new)
        vn16 = v_new.astype(BF16)
        outs[c] = dot(qg[r], s16) + dot(qa[r], vn16)
        state = state * decay[c] + dot(kd_t_chunks[c], vn16)
    return state, jnp.concatenate(outs, axis=0)


def _dn_scan_kernel(kf_ref, vf_ref, qf_ref, rf_ref, kb_ref, vb_ref, qb_ref, rb_ref, s0_ref,
                    of_ref, ob_ref, sfin_ref, state):
    i = pl.program_id(2)
    tl = kf_ref.shape[0]
    n_groups = tl // DN_GROUP

    @pl.when(i == 0)
    def _():
        state[...] = s0_ref[0, 0]

    def group(gi, carry):
        s_f, s_b = carry
        rf = pl.ds(pl.multiple_of(gi * DN_GROUP, DN_GROUP), DN_GROUP)
        rb = pl.ds(pl.multiple_of((n_groups - 1 - gi) * DN_GROUP, DN_GROUP), DN_GROUP)
        grp_f = _dn_group(kf_ref[rf, :], vf_ref[rf, :], qf_ref[rf, :], rf_ref[0, :, rf], 0)
        grp_b = _dn_group(kb_ref[rb, :], vb_ref[rb, :], qb_ref[rb, :], rb_ref[0, :, rb], 1)
        s_f, o_f = _dn_scan_group(s_f, grp_f, 0)
        s_b, o_b = _dn_scan_group(s_b, grp_b, 1)
        of_ref[rf, :] = o_f
        ob_ref[rb, :] = o_b
        return s_f, s_b

    s_f, s_b = lax.fori_loop(0, n_groups, group, (state[0], state[1]))
    state[0] = s_f
    state[1] = s_b

    @pl.when(i == pl.num_programs(2) - 1)
    def _():
        sfin_ref[0, 0, 0] = s_f
        sfin_ref[0, 0, 1] = s_b


def _dn_scan(q, k, v, rows, s0, B, L):
    T = B * L
    H, dk = DN_HEADS, DN_HEAD_DIM
    tl = min(DN_SCAN_TOKENS_PER_STEP, L)
    assert L % tl == 0 and tl % DN_GROUP == 0
    nb = L // tl
    fwd = lambda b, h, i: (b * nb + i, h)
    bwd = lambda b, h, i: (b * nb + nb - 1 - i, h)
    tok = lambda m: pl.BlockSpec((tl, dk), m)
    row_f = pl.BlockSpec((1, 8, tl), lambda b, h, i: (h, 0, b * nb + i))
    row_b = pl.BlockSpec((1, 8, tl), lambda b, h, i: (h, 0, b * nb + nb - 1 - i))
    st = pl.BlockSpec((1, 1, 2, dk, dk), lambda b, h, i: (b, h, 0, 0, 0))
    return pl.pallas_call(
        _dn_scan_kernel,
        grid=(B, H, nb),
        in_specs=[tok(fwd), tok(fwd), tok(fwd), row_f, tok(bwd), tok(bwd), tok(bwd), row_b, st],
        out_specs=[tok(fwd), tok(bwd), st],
        out_shape=[jax.ShapeDtypeStruct((T, DN_INNER), jnp.float32)] * 2
        + [jax.ShapeDtypeStruct((B, H, 2, dk, dk), jnp.float32)],
        scratch_shapes=[pltpu.VMEM((2, dk, dk), jnp.float32)],
        compiler_params=pltpu.CompilerParams(dimension_semantics=("arbitrary", "arbitrary", "arbitrary"),
                                             vmem_limit_bytes=48 * 1024 * 1024),
        name="dn_scan",
    )(k, v, q, rows, k, v, q, rows, s0)


def _dn_outproj_kernel(of_ref, ob_ref, gate_ref, gn_ref, w_ref, h_ref, rg_ref, o_ref):
    parts = []
    for head in range(DN_HEADS):
        c = slice(head * DN_HEAD_DIM, (head + 1) * DN_HEAD_DIM)
        o = of_ref[:, c] + ob_ref[:, c]
        on = o * lax.rsqrt(jnp.mean(o * o, axis=-1, keepdims=True) + EPS) * gn_ref[...]
        gt = gate_ref[:, c]
        parts.append((on * (gt * (1.0 / (1.0 + jnp.exp(-gt))))).astype(BF16))
    y = jnp.concatenate(parts, axis=1)
    o_ref[...] = h_ref[...] + rg_ref[0] * jnp.dot(y, w_ref[...], preferred_element_type=jnp.float32)


def _dn_outproj(o_f, o_b, gate, g_norm, w_out, h, res_gate):
    T, D = h.shape
    tb = PROJ_TOKENS_PER_STEP
    steps_per_batch = T // res_gate.shape[0] // tb
    tok = pl.BlockSpec((tb, D), lambda i: (i, 0))
    gn = g_norm.reshape(1, DN_HEAD_DIM).astype(jnp.float32)
    w16 = w_out.astype(BF16)
    return pl.pallas_call(
        _dn_outproj_kernel,
        grid=(T // tb,),
        in_specs=[tok, tok, tok, pl.BlockSpec(gn.shape, lambda i: (0, 0)), pl.BlockSpec(w16.shape, lambda i: (0, 0)),
                  tok, pl.BlockSpec((1, 1, D), lambda i: (i // steps_per_batch, 0, 0))],
        out_specs=tok,
        out_shape=jax.ShapeDtypeStruct((T, D), h.dtype),
        compiler_params=pltpu.CompilerParams(dimension_semantics=("arbitrary",)),
        name="dn_outproj",
    )(o_f, o_b, gate, gn, w16, h, res_gate)


def _deltanet_residual(h, hc, g, shift, scale, shift_c, scale_c, res_gate,
                       w_in, conv_w, a_log, dt_bias, o_norm_g, w_out):
    B, L, D = h.shape
    Lc = hc.shape[1]
    sh_c = jnp.broadcast_to(shift_c, (B, 1, D))
    sc_c = jnp.broadcast_to(scale_c, (B, 1, D))
    qc, kc, vc, _, rows_c = _dn_inproj(hc, g, sh_c, sc_c, w_in, conv_w, a_log, dt_bias, row_len=Lc)
    zeros = jnp.zeros((B, DN_HEADS, 2, DN_HEAD_DIM, DN_HEAD_DIM), jnp.float32)
    _, _, s_ctx = _dn_scan(qc, kc, vc, rows_c, zeros, B, Lc)
    q, k, v, gate, rows = _dn_inproj(h, g, shift, scale, w_in, conv_w, a_log, dt_bias, row_len=GRID_W)
    o_f, o_b, _ = _dn_scan(q, k, v, rows, s_ctx, B, L)
    return _dn_outproj(o_f, o_b, gate, o_norm_g, w_out, h.reshape(B * L, D), res_gate).reshape(B, L, D)


def _hy_inproj_kernel(h_ref, g_ref, sh_ref, sc_ref, w_ref, conv_ref, bias_ref, v_ref, x1_ref, x2_ref, *, row_len):
    xb = _norm_modulate(h_ref, g_ref, sh_ref, sc_ref).astype(BF16)
    pos = lax.broadcasted_iota(jnp.int32, (xb.shape[0], 1), 0) % row_len
    for part, out_ref in enumerate((v_ref, x1_ref, x2_ref)):
        for tile in range(HY_WIDTH // LANES):
            c0 = part * HY_WIDTH + tile * LANES
            z = jnp.dot(xb, w_ref[:, c0:c0 + LANES], preferred_element_type=jnp.float32)
            y = _row_conv(z, conv_ref[:, c0:c0 + LANES], pos, row_len) + bias_ref[:, c0:c0 + LANES]
            out_ref[:, tile * LANES:(tile + 1) * LANES] = y


def _hy_inproj(h, g, shift, scale, w_in, conv_w, conv_b, row_len):
    B, L, D = h.shape
    T = B * L
    tb = min(PROJ_TOKENS_PER_STEP, L)
    assert L % tb == 0 and tb % row_len == 0
    steps_per_batch = L // tb
    tok_spec = lambda n: pl.BlockSpec((tb, n), lambda i: (i, 0))
    full = lambda a: pl.BlockSpec(a.shape, lambda i: (0,) * a.ndim)
    mod_spec = pl.BlockSpec((1, 1, D), lambda i: (i // steps_per_batch, 0, 0))
    g2, w16, bias = g.reshape(1, D), w_in.astype(BF16), conv_b.reshape(1, -1)
    return pl.pallas_call(
        functools.partial(_hy_inproj_kernel, row_len=row_len),
        grid=(T // tb,),
        in_specs=[tok_spec(D), full(g2), mod_spec, mod_spec, full(w16), full(conv_w), full(bias)],
        out_specs=[tok_spec(HY_WIDTH)] * 3,
        out_shape=[jax.ShapeDtypeStruct((T, HY_WIDTH), jnp.float32)] * 3,
        compiler_params=pltpu.CompilerParams(dimension_semantics=("arbitrary",),
                                             vmem_limit_bytes=48 * 1024 * 1024),
        name="hy_inproj",
    )(h.reshape(T, D), g2, shift, scale, w16, conv_w, bias)


def _residual_proj_kernel(z_ref, w_ref, h_ref, rg_ref, o_ref):
    o_ref[...] = h_ref[...] + rg_ref[0] * jnp.dot(z_ref[...].astype(BF16), w_ref[...],
                                                   preferred_element_type=jnp.float32)


def _residual_proj(z, w_out, h, res_gate):
    T, D = h.shape
    tb = PROJ_TOKENS_PER_STEP
    steps_per_batch = T // res_gate.shape[0] // tb
    w16 = w_out.astype(BF16)
    return pl.pallas_call(
        _residual_proj_kernel,
        grid=(T // tb,),
        in_specs=[pl.BlockSpec((tb, z.shape[1]), lambda i: (i, 0)), pl.BlockSpec(w16.shape, lambda i: (0, 0)),
                  pl.BlockSpec((tb, D), lambda i: (i, 0)),
                  pl.BlockSpec((1, 1, D), lambda i: (i // steps_per_batch, 0, 0))],
        out_specs=pl.BlockSpec((tb, D), lambda i: (i, 0)),
        out_shape=jax.ShapeDtypeStruct((T, D), h.dtype),
        compiler_params=pltpu.CompilerParams(dimension_semantics=("arbitrary",)),
        name="residual_proj",
    )(z, w16, h, res_gate)


SUBLANES = 8
FFT_P = 128
FFT_RA, FFT_RB = 8, 16
assert FFT_RA * FFT_RB == FFT_P
FFT_COLS_PER_STEP = 4


def _cmul_const(v, w):
    re, im = v
    wr, wi = round(w.real, 15), round(w.imag, 15)
    if wi == 0.0:
        return (re, im) if wr == 1.0 else ((-re, -im) if wr == -1.0 else (re * wr, im * wr))
    if wr == 0.0:
        return (-im, re) if wi == 1.0 else ((im, -re) if wi == -1.0 else (-im * wi, re * wi))
    return re * wr - im * wi, re * wi + im * wr


def _cmul(v, w):
    return v[0] * w[0] - v[1] * w[1], v[0] * w[1] + v[1] * w[0]


def _small_dft(xs, sign):
    n = len(xs)
    if n == 1:
        return xs
    even, odd = _small_dft(xs[0::2], sign), _small_dft(xs[1::2], sign)
    out = [None] * n
    for k in range(n // 2):
        t = _cmul_const(odd[k], cmath.exp(sign * 2j * math.pi * k / n))
        out[k] = (even[k][0] + t[0], even[k][1] + t[1])
        out[k + n // 2] = (even[k][0] - t[0], even[k][1] - t[1])
    return out


def _dft_p(load, store, tmp_re, tmp_im, inverse):
    sign = 1 if inverse else -1
    first, second = (FFT_RB, FFT_RA) if inverse else (FFT_RA, FFT_RB)
    for o in range(second):
        src = [load(FFT_RB * o + i) for i in range(first)] if inverse else [load(FFT_RB * i + o) for i in range(first)]
        for i, y in enumerate(_small_dft(src, sign)):
            b, c = (i, o) if inverse else (o, i)
            y = _cmul_const(y, cmath.exp(sign * 2j * math.pi * b * c / FFT_P))
            tmp_re[FFT_RB * c + b] = y[0]
            tmp_im[FFT_RB * c + b] = y[1]
    for o in range(first):
        idx = [FFT_RB * i + o for i in range(second)] if inverse else [FFT_RB * o + i for i in range(second)]
        for i, y in enumerate(_small_dft([(tmp_re[p], tmp_im[p]) for p in idx], sign)):
            store(FFT_RB * i + o if inverse else FFT_RB * o + i, y)


def _fft_cols_kernel(tw_ref, xr_ref, xi_ref, yr_ref, yi_ref, tmp_re, tmp_im, *, inverse):
    cols = yr_ref.shape[-3]
    col0 = pl.program_id(1) * cols

    def column(j, carry):
        if inverse:
            load = lambda n: (xr_ref[0, n, j], xi_ref[0, n, j])

            def store(n, y):
                if n < FFT_P // 2:
                    yr_ref[0, 0, n, j] = y[0]
                    yi_ref[0, 0, n, j] = y[1]
        else:
            load = lambda n: (xr_ref[0, 0, n, j], xi_ref[0, 0, n, j])

            def store(p, y):
                y = _cmul(y, (tw_ref[0, p, col0 + j], tw_ref[1, p, col0 + j]))
                yr_ref[0, p, j] = y[0]
                yi_ref[0, p, j] = y[1]
        _dft_p(load, store, tmp_re, tmp_im, inverse)
        return carry

    lax.fori_loop(0, cols, column, 0)


def _fft_rows_kernel(tw_ref, xr_ref, xi_ref, hr_ref, hi_ref, yr_ref, yi_ref, tmp_re, tmp_im, mid_re, mid_im, *,
                     convolve):
    rows = yr_ref.shape[1]
    row0 = pl.program_id(1) * rows

    def row(r, carry):
        load = lambda n: (xr_ref[0, r, n], xi_ref[0, r, n])
        if not convolve:
            def store(q, y):
                yr_ref[0, r, q] = y[0]
                yi_ref[0, r, q] = y[1]
            _dft_p(load, store, tmp_re, tmp_im, False)
            return carry

        def store_mid(q, y):
            y = _cmul(y, (hr_ref[0, r, q], hi_ref[0, r, q]))
            mid_re[q] = y[0]
            mid_im[q] = y[1]
        _dft_p(load, store_mid, tmp_re, tmp_im, False)

        def store(n, y):
            y = _cmul(y, (tw_ref[0, row0 + r, n], -tw_ref[1, row0 + r, n]))
            yr_ref[0, r, n] = y[0]
            yi_ref[0, r, n] = y[1]
        _dft_p(lambda q: (mid_re[q], mid_im[q]), store, tmp_re, tmp_im, True)
        return carry

    lax.fori_loop(0, rows, row, 0)


def _fft_twiddles():
    p = jnp.arange(FFT_P)
    k1 = p // FFT_RB + FFT_RA * (p % FFT_RB)
    ang = (2 * math.pi / (FFT_P * FFT_P)) * (k1[:, None] * jnp.arange(FFT_P)[None, :]).astype(jnp.float32)
    return jnp.stack([jnp.cos(ang), -jnp.sin(ang)]).astype(jnp.float32)


_TILE = (SUBLANES, LANES)
_FFT_PARAMS = dict(dimension_semantics=("arbitrary", "arbitrary"), vmem_limit_bytes=56 * 1024 * 1024)
_TMP = pltpu.VMEM((FFT_P,) + _TILE, jnp.float32)


def _fft_forward_cols(x, tw):
    G = x.shape[1]
    j = FFT_COLS_PER_STEP
    in_spec = lambda ri: pl.BlockSpec((1, 1, FFT_P, j) + _TILE, lambda g, c: (ri, g, 0, c, 0, 0))
    out_spec = pl.BlockSpec((1, FFT_P, j) + _TILE, lambda g, c: (g, 0, c, 0, 0))
    out = jax.ShapeDtypeStruct((G, FFT_P, FFT_P) + _TILE, jnp.float32)
    return pl.pallas_call(
        functools.partial(_fft_cols_kernel, inverse=False),
        grid=(G, FFT_P // j),
        in_specs=[pl.BlockSpec(memory_space=pltpu.SMEM), in_spec(0), in_spec(1)],
        out_specs=[out_spec, out_spec], out_shape=[out, out], scratch_shapes=[_TMP, _TMP],
        compiler_params=pltpu.CompilerParams(**_FFT_PARAMS), name="fft_fwd_cols",
    )(tw, x, x)


def _fft_rows(yr, yi, tw, hr=None, hi=None):
    G = yr.shape[0]
    k = FFT_COLS_PER_STEP
    spec = pl.BlockSpec((1, k, FFT_P) + _TILE, lambda g, r: (g, r, 0, 0, 0))
    out = jax.ShapeDtypeStruct(yr.shape, jnp.float32)
    convolve = hr is not None
    if not convolve:
        hr, hi = yr, yi
    return pl.pallas_call(
        functools.partial(_fft_rows_kernel, convolve=convolve),
        grid=(G, FFT_P // k),
        in_specs=[pl.BlockSpec(memory_space=pltpu.SMEM), spec, spec, spec, spec],
        out_specs=[spec, spec], out_shape=[out, out], scratch_shapes=[_TMP] * 4,
        compiler_params=pltpu.CompilerParams(**_FFT_PARAMS), name="fft_rows",
    )(tw, yr, yi, hr, hi)


def _fft_inverse_cols(yr, yi, tw):
    G = yr.shape[0]
    j = FFT_COLS_PER_STEP
    in_spec = pl.BlockSpec((1, FFT_P, j) + _TILE, lambda g, c: (g, 0, c, 0, 0))
    part = jax.ShapeDtypeStruct((1, G, FFT_P // 2, FFT_P) + _TILE, jnp.float32)
    o_spec = pl.BlockSpec((1, 1, FFT_P // 2, j) + _TILE, lambda g, c: (0, g, 0, c, 0, 0))
    re, im = pl.pallas_call(
        functools.partial(_fft_cols_kernel, inverse=True),
        grid=(G, FFT_P // j),
        in_specs=[pl.BlockSpec(memory_space=pltpu.SMEM), in_spec, in_spec],
        out_specs=[o_spec, o_spec], out_shape=[part, part], scratch_shapes=[_TMP, _TMP],
        compiler_params=pltpu.CompilerParams(**_FFT_PARAMS), name="fft_inv_cols",
    )(tw, yr, yi)
    return jnp.concatenate([re, im], axis=0)


def _to_fft_layout(u):
    B, L, C = u.shape
    half, ch = B // 2, SUBLANES // (B // 2)
    G = C // (ch * LANES)
    assert 2 * L == FFT_P * FFT_P and half * ch == SUBLANES and G * ch * LANES == C
    t = u.reshape(2, half, L, G, ch, LANES).transpose(0, 3, 2, 1, 4, 5).reshape(2, G, L, SUBLANES, LANES)
    t = jnp.pad(t, ((0, 0), (0, 0), (0, L), (0, 0), (0, 0)))
    return t.reshape(2, G, FFT_P, FFT_P, SUBLANES, LANES)


def _from_fft_layout(t, B, L, C):
    half, ch = B // 2, SUBLANES // (B // 2)
    G = C // (ch * LANES)
    return t.reshape(2, G, L, half, ch, LANES).transpose(0, 3, 2, 1, 4, 5).reshape(B, L, C)


def _filter_spectra(filt, tw, B):
    _, _, L, C = filt.shape
    half, ch = B // 2, SUBLANES // (B // 2)
    G = C // (ch * LANES)
    n = 2 * L
    kern = jnp.concatenate([filt[:, 0], jnp.zeros((2, 1, C), jnp.float32), jnp.flip(filt[:, 1, 1:], axis=1)], axis=1)
    t = kern.reshape(2, n, G, ch, LANES).transpose(2, 1, 0, 3, 4)
    t = jnp.pad(t, ((0, 0), (0, 0), (0, half - 2), (0, 0), (0, 0))).reshape(G, n, SUBLANES, LANES)
    x = jnp.stack([t, jnp.zeros_like(t)]).reshape(2, G, FFT_P, FFT_P, SUBLANES, LANES)
    fr, fi = _fft_rows(*_fft_forward_cols(x, tw), tw)
    spectra = []
    for order in range(2):
        pick = lambda a: jnp.broadcast_to(
            a.reshape(G, FFT_P, FFT_P, half, ch, LANES)[:, :, :, order:order + 1] * (1.0 / n),
            (G, FFT_P, FFT_P, half, ch, LANES)).reshape(G, FFT_P, FFT_P, SUBLANES, LANES)
        spectra.append((pick(fr), pick(fi)))
    return spectra


def _long_conv_fft(u, spectrum, tw):
    B, L, C = u.shape
    yr, yi = _fft_forward_cols(_to_fft_layout(u), tw)
    yr, yi = _fft_rows(yr, yi, tw, *spectrum)
    return _from_fft_layout(_fft_inverse_cols(yr, yi, tw), B, L, C)


def _hyena_residual(h, g, shift, scale, res_gate, w_in, conv_w, conv_b, f_w1, f_b1, f_w2, f_b2, f_w3, f_b3, f_w4,
                    freq, skip, w_out):
    B, L, D = h.shape
    v, x1, x2 = [a.reshape(B, L, HY_WIDTH) for a in _hy_inproj(h, g, shift, scale, w_in, conv_w, conv_b, GRID_W)]
    filt = _hyena_filters(L, f_w1, f_b1, f_w2, f_b2, f_w3, f_b3, f_w4, freq)
    tw = _fft_twiddles()
    spec1, spec2 = _filter_spectra(filt, tw, B)
    z1 = x1 * (_long_conv_fft(v, spec1, tw) + v * skip[0])
    z2 = x2 * (_long_conv_fft(z1, spec2, tw) + z1 * skip[1])
    return _residual_proj(z2.reshape(B * L, HY_WIDTH), w_out, h.reshape(B * L, D), res_gate).reshape(B, L, D)


def _hyena_filters(L, w1, b1, w2, b2, w3, b3, w4, freq):
    f32 = jnp.float32
    D = w4.shape[-1] // 4
    pos = jnp.arange(L, dtype=f32)
    t = pos / max(L - 1, 1)
    ang = (2 * math.pi * pos / L)[:, None] * jnp.linspace(1e-4, HY_BANDS - 1, HY_BANDS, dtype=f32)[None]
    z = jnp.concatenate([t[:, None], jnp.cos(ang), -jnp.sin(ang)], axis=-1)
    fr = freq.astype(f32)
    hid = jnp.sin(fr * (z @ w1.astype(f32) + b1.astype(f32)))
    hid = jnp.sin(fr * (hid @ w2.astype(f32) + b2.astype(f32)))
    hid = jnp.sin(fr * (hid @ w3.astype(f32) + b3.astype(f32)))
    h = hid @ w4.astype(f32)
    deltas = jnp.abs(jnp.linspace(math.log(HY_TARGET) / HY_FAST, math.log(HY_TARGET) / HY_SLOW, D, dtype=f32))
    decay = jnp.exp(-t[:, None] * deltas[None])
    return (h.reshape(L, 2, 2, D) * decay[:, None, None, :]).transpose(1, 2, 0, 3)


def _long_conv(u, hf, hb, d):
    B, L, D = u.shape
    filt2 = jnp.concatenate([hf, jnp.zeros((1, D), jnp.float32), jnp.flip(hb[1:], axis=0)], axis=0)
    uf = jnp.fft.rfft(u.astype(jnp.float32), n=2 * L, axis=1)
    y = jnp.fft.irfft(uf * jnp.fft.rfft(filt2, axis=0)[None], n=2 * L, axis=1)[:, :L]
    return (y + u.astype(jnp.float32) * d.astype(jnp.float32)).astype(u.dtype)


def _hyena(hn, n_rows, w_in, conv_w, conv_b, f_w1, f_b1, f_w2, f_b2, f_w3, f_b3, f_w4, freq, skip, w_out):
    B, L, _ = hn.shape
    z = _short_conv(hn @ w_in, conv_w, n_rows) + conv_b.astype(hn.dtype)
    v, x1, x2 = jnp.split(z, 3, axis=-1)
    filt = _hyena_filters(L, f_w1, f_b1, f_w2, f_b2, f_w3, f_b3, f_w4, freq)
    z1 = x1 * _long_conv(v, filt[0, 0], filt[0, 1], skip[0])
    z2 = x2 * _long_conv(z1, filt[1, 0], filt[1, 1], skip[1])
    return z2 @ w_out


PEER_SLOTS = PEER_HEADS * PEER_TOPK
PEER_TOKENS_PER_STEP = 64
PEER_GATHER_BUFFERS = 4


def _peer_expert_kernel(idx_ref, idx_next_ref, x_ref, gt_ref, h_ref, res_gate_ref, uv_ref, o_ref, buf, sem):
    tb, d = x_ref.shape
    nbuf = buf.shape[0]
    step = pl.program_id(0)
    last_step = pl.num_programs(0) - 1

    def row_copy(row, j, slot):
        return pltpu.make_async_copy(uv_ref.at[row], buf.at[slot, pl.ds(j, 1)], sem.at[slot])

    def start_token(ids_ref, t, slot):
        for j in range(PEER_SLOTS):
            row_copy(ids_ref[t, j], j, slot).start(priority=j % 2)

    def wait_token(slot):
        for j in range(PEER_SLOTS):
            row_copy(0, j, slot).wait()

    @pl.when(step == 0)
    def _():
        for t in range(nbuf - 1):
            start_token(idx_ref, t, t)

    lane = lax.broadcasted_iota(jnp.int32, (PEER_SLOTS, LANES), 1)
    lane0 = (step * tb) % LANES

    def finish_token(t, slot):
        wait_token(slot)
        x_row = x_ref[pl.ds(t, 1), :]
        hid = jnp.sum(buf[slot, :, :d] * x_row, axis=-1, keepdims=True)
        gate = jnp.sum(jnp.where(lane == lane0 + t, gt_ref[...], 0.0), axis=-1, keepdims=True)
        coef = 0.5 * hid * (1.0 + lax.erf(hid * (2.0 ** -0.5))) * gate
        y = jnp.sum(coef * buf[slot, :, d:], axis=0, keepdims=True)
        o_ref[pl.ds(t, 1), :] = h_ref[pl.ds(t, 1), :] + res_gate_ref[0] * y

    def token_group(g, carry):
        for slot in range(nbuf):
            t = g * nbuf + slot
            start_token(idx_ref, t + nbuf - 1, (slot + nbuf - 1) % nbuf)
            finish_token(t, slot)
        return carry

    lax.fori_loop(0, tb // nbuf - 1, token_group, 0)
    for slot in range(nbuf):
        t = tb - nbuf + slot
        ahead = t + nbuf - 1
        if ahead < tb:
            start_token(idx_ref, ahead, ahead % nbuf)
        else:
            start_token(idx_next_ref, ahead - tb, ahead % nbuf)
        finish_token(t, slot)

    @pl.when(step == last_step)
    def _():
        for slot in range(nbuf - 1):
            wait_token(slot)


def _peer_experts(xt, idx_t, gate_t, h, res_gate, u_tab, v_tab):
    T, D = xt.shape
    tb = PEER_TOKENS_PER_STEP
    nsteps = T // tb
    steps_per_batch = nsteps // res_gate.shape[0]
    assert T % LANES == 0 and LANES % tb == 0 and tb % PEER_GATHER_BUFFERS == 0
    uv = jnp.concatenate([u_tab, v_tab], axis=1)[:, None, :]
    steps_per_gate_tile = LANES // tb
    idx = idx_t.T
    return pl.pallas_call(
        _peer_expert_kernel,
        grid=(nsteps,),
        in_specs=[pl.BlockSpec((tb, PEER_SLOTS), lambda i: (i, 0), memory_space=pltpu.SMEM),
                  pl.BlockSpec((tb, PEER_SLOTS), lambda i: (jnp.minimum(i + 1, nsteps - 1), 0),
                               memory_space=pltpu.SMEM),
                  pl.BlockSpec((tb, D), lambda i: (i, 0)),
                  pl.BlockSpec((PEER_SLOTS, LANES), lambda i: (0, i // steps_per_gate_tile)),
                  pl.BlockSpec((tb, D), lambda i: (i, 0)),
                  pl.BlockSpec((1, 1, D), lambda i: (i // steps_per_batch, 0, 0)),
                  pl.BlockSpec(memory_space=pl.ANY)],
        out_specs=pl.BlockSpec((tb, D), lambda i: (i, 0)),
        out_shape=jax.ShapeDtypeStruct((T, D), xt.dtype),
        scratch_shapes=[pltpu.VMEM((PEER_GATHER_BUFFERS, PEER_SLOTS, 2 * D), jnp.float32),
                        pltpu.SemaphoreType.DMA((PEER_GATHER_BUFFERS,))],
        compiler_params=pltpu.CompilerParams(dimension_semantics=("arbitrary",)),
        name="peer_experts",
    )(idx, idx, xt, gate_t, h, res_gate, uv)


ROUTE_TOKENS_PER_STEP = 256
assert PEER_TOPK == 16


def _top_rows(s, order, payload, n):
    vals, outs = [], []
    for _ in range(n):
        m = jnp.max(s, axis=0, keepdims=True)
        first = jnp.min(jnp.where(s == m, order, jnp.inf), axis=0, keepdims=True)
        hit = order == first
        vals.append(m)
        outs.append(jnp.max(jnp.where(hit, payload, -1.0), axis=0, keepdims=True))
        s = jnp.where(hit, -jnp.inf, s)
    return jnp.concatenate(vals, axis=0), jnp.concatenate(outs, axis=0)


def _candidate_blocks(s1, i1, s2, i2):
    k = PEER_TOPK
    c = s1.shape[1]
    r8 = lax.broadcasted_iota(jnp.int32, (8, c), 0).astype(jnp.float32)
    r16 = lax.broadcasted_iota(jnp.int32, (k, c), 0).astype(jnp.float32)
    ninf = -jnp.inf

    def col(i, rows, r, keep):
        return (jnp.where(keep, s1[i:i + 1] + s2[:rows], ninf), i * k + r, i1[i:i + 1] * PEER_NKEYS + i2[:rows])

    def row(j, rows, r, keep):
        return (jnp.where(keep, s1[:rows] + s2[j:j + 1], ninf), r * k + j, i1[:rows] * PEER_NKEYS + i2[j:j + 1])

    blocks = [row(0, k, r16, r16 >= 0), col(0, k, r16, r16 >= 1), col(1, 8, r8, r8 >= 1), row(1, 8, r8, r8 >= 2),
              row(2, 8, r8, (r8 >= 2) & (r8 <= 4)), row(3, 8, r8, (r8 >= 2) & (r8 <= 3)), row(4, 8, r8, r8 == 2)]
    return [jnp.concatenate([b[n] for b in blocks], axis=0) for n in range(3)]


def _peer_route_kernel(h_ref, g_ref, sh_ref, sc_ref, wq_ref, keys_ref, xn_ref, idx_ref, gate_ref):
    x = h_ref[...]
    xn = x * lax.rsqrt(jnp.mean(x * x, axis=-1, keepdims=True) + EPS) * g_ref[...]
    xm = xn * (1 + sc_ref[0]) + sh_ref[0]
    xn_ref[...] = xm
    xb = xm.astype(jnp.bfloat16)
    tb = x.shape[0]
    dk2 = PEER_DK // 2
    key_order = lax.broadcasted_iota(jnp.int32, (PEER_NKEYS, tb), 0).astype(jnp.float32)

    def head(h, carry):
        q = jnp.dot(xb, wq_ref[h], preferred_element_type=jnp.float32)
        tops = []
        for p in range(2):
            qp = q[:, p * dk2:(p + 1) * dk2].astype(jnp.bfloat16)
            s = lax.dot_general(keys_ref[h, p], qp, (((1,), (1,)), ((), ())),
                                preferred_element_type=jnp.float32)
            tops.append(_top_rows(s, key_order, key_order, PEER_TOPK))
        (s1, i1), (s2, i2) = tops
        cand, order, expert = _candidate_blocks(s1, i1, s2, i2)
        top_s, top_e = _top_rows(cand, order, expert, PEER_TOPK)
        e = jnp.exp(top_s - top_s[0:1])
        rows = pl.ds(pl.multiple_of(h * PEER_TOPK, PEER_TOPK), PEER_TOPK)
        gate_ref[rows, :] = e / jnp.sum(e, axis=0, keepdims=True)
        idx_ref[rows, :] = top_e.astype(jnp.int32)
        return carry

    lax.fori_loop(0, PEER_HEADS, head, 0)


def _peer_route(h, g, shift, scale, w_q, sub_keys):
    B, L, D = h.shape
    T = B * L
    tb = min(ROUTE_TOKENS_PER_STEP, L)
    assert L % tb == 0 and tb % LANES == 0
    steps_per_batch = L // tb
    dk2 = PEER_DK // 2
    wq = w_q.reshape(D, PEER_HEADS, 2 * dk2).transpose(1, 0, 2).astype(jnp.bfloat16)
    keys = sub_keys.astype(jnp.bfloat16)
    return pl.pallas_call(
        _peer_route_kernel,
        grid=(T // tb,),
        in_specs=[pl.BlockSpec((tb, D), lambda i: (i, 0)),
                  pl.BlockSpec((1, D), lambda i: (0, 0)),
                  pl.BlockSpec((1, 1, D), lambda i: (i // steps_per_batch, 0, 0)),
                  pl.BlockSpec((1, 1, D), lambda i: (i // steps_per_batch, 0, 0)),
                  pl.BlockSpec((PEER_HEADS, D, 2 * dk2), lambda i: (0, 0, 0)),
                  pl.BlockSpec((PEER_HEADS, 2, PEER_NKEYS, dk2), lambda i: (0, 0, 0, 0))],
        out_specs=[pl.BlockSpec((tb, D), lambda i: (i, 0)),
                   pl.BlockSpec((PEER_SLOTS, tb), lambda i: (0, i)),
                   pl.BlockSpec((PEER_SLOTS, tb), lambda i: (0, i))],
        out_shape=[jax.ShapeDtypeStruct((T, D), h.dtype),
                   jax.ShapeDtypeStruct((PEER_SLOTS, T), jnp.int32),
                   jax.ShapeDtypeStruct((PEER_SLOTS, T), jnp.float32)],
        compiler_params=pltpu.CompilerParams(dimension_semantics=("arbitrary",),
                                             vmem_limit_bytes=48 * 1024 * 1024),
        name="peer_route",
    )(h.reshape(T, D), g.reshape(1, D), shift, scale, wq, keys)


def _peer_residual(h, g, shift, scale, res_gate, w_q, sub_keys, u_tab, v_tab):
    B, L, D = h.shape
    xn, idx_t, gate_t = _peer_route(h, g, shift, scale, w_q, sub_keys)
    return _peer_experts(xn, idx_t, gate_t, h.reshape(B * L, D), res_gate, u_tab, v_tab).reshape(B, L, D)


def _final_rmsnorm_kernel(h_ref, g_ref, o_ref):
    x = h_ref[...]
    y = x * lax.rsqrt(jnp.mean(x * x, axis=-1, keepdims=True) + EPS)
    o_ref[...] = y * g_ref[...]


def _final_rmsnorm(h, g):
    B, L, D = h.shape
    T = B * L
    tm = 1024
    out = pl.pallas_call(
        _final_rmsnorm_kernel,
        grid=(T // tm,),
        in_specs=[pl.BlockSpec((tm, D), lambda i: (i, 0)),
                  pl.BlockSpec((1, D), lambda i: (0, 0))],
        out_specs=pl.BlockSpec((tm, D), lambda i: (i, 0)),
        out_shape=jax.ShapeDtypeStruct((T, D), h.dtype),
        name="final_rmsnorm",
    )(h.reshape(T, D), g.reshape(1, D))
    return out.reshape(B, L, D)


def kernel(x, c, ctx, c_ctx, ada_w, ada_b, norm_g,
           dn_w_in, dn_conv_w, dn_a_log, dn_dt_bias, dn_norm_g, dn_w_out,
           hy_w_in, hy_conv_w, hy_conv_b, hy_f_w1, hy_f_b1, hy_f_w2, hy_f_b2,
           hy_f_w3, hy_f_b3, hy_f_w4, hy_freq, hy_skip, hy_w_out,
           peer_w_q, peer_keys, peer_u, peer_v, final_g):
    f32 = jnp.float32
    B, L, D = x.shape
    rows = L // GRID_W
    silu_c = jax.nn.silu(c.astype(f32))
    silu_cc = jax.nn.silu(c_ctx.astype(f32))
    h, hc = x, ctx
    for i in range(DEPTH):
        kind = i % N_MIXERS
        j = i // N_MIXERS
        assert not any(l % N_MIXERS == MIXER_DELTA for l in range(i + 1, DEPTH))
        w_ada = ada_w[i].astype(f32)
        b_ada = ada_b[i].astype(f32)
        mod = (silu_c @ w_ada + b_ada).astype(h.dtype).reshape(B, 6, 1, D)
        sh1, sc1, gt1, sh2, sc2, gt2 = [mod[:, m] for m in range(6)]
        if kind == MIXER_DELTA:
            modc = (silu_cc @ w_ada + b_ada).astype(hc.dtype).reshape(6, 1, 1, D)
            h = _deltanet_residual(h, hc, norm_g[i, 0], sh1, sc1, modc[0], modc[1], gt1, dn_w_in[j], dn_conv_w[j],
                                   dn_a_log[j], dn_dt_bias[j], dn_norm_g[j], dn_w_out[j])
        else:
            hy_args = (hy_w_in[j], hy_conv_w[j], hy_conv_b[j], hy_f_w1[j], hy_f_b1[j], hy_f_w2[j],
                       hy_f_b2[j], hy_f_w3[j], hy_f_b3[j], hy_f_w4[j], hy_freq[j], hy_skip[j], hy_w_out[j])
            h = _hyena_residual(h, norm_g[i, 0], sh1, sc1, gt1, *hy_args)
        peer_args = (peer_w_q[i], peer_keys[i], peer_u[i], peer_v[i])
        h = _peer_residual(h, norm_g[i, 1], sh2, sc2, gt2, *peer_args)
    return _final_rmsnorm(h, final_g)
```

```python
import cmath
import functools
import math
import jax
import jax.numpy as jnp
from jax import lax
from jax.experimental import pallas as pl
from jax.experimental.pallas import tpu as pltpu

D_MODEL = 1024
BATCH = 8
SEQ = 8192
DEPTH = 2

GRID_W = 64
CTX_LEN = 256

N_MIXERS = 2
MIXER_DELTA = 0
MIXER_HYENA = 1

EPS = 1e-6

DN_HEAD_DIM = 128
DN_HEADS = D_MODEL // DN_HEAD_DIM
DN_INNER = DN_HEADS * DN_HEAD_DIM
DN_CONV = 5
DN_CHUNK = 64
DN_CHUNK_LOG2 = 6
DN_PROJ = 4 * DN_INNER + 4 * DN_HEADS

HY_WIDTH = D_MODEL
HY_SHORT = 3
HY_BANDS = 16
HY_EMB = 1 + 2 * HY_BANDS
HY_FF = 64
HY_FAST = 0.3
HY_SLOW = 1.5
HY_TARGET = 1e-2

PEER_HEADS = 8
PEER_NKEYS = 128
PEER_EXPERTS = PEER_NKEYS * PEER_NKEYS
PEER_DK = 256
PEER_TOPK = 16
PEER_BLOCK = 128


def _rmsnorm(x, g):
    xf = x.astype(jnp.float32)
    y = xf * lax.rsqrt(jnp.mean(xf * xf, axis=-1, keepdims=True) + EPS)
    return (y * g.astype(jnp.float32)).astype(x.dtype)


def _modulate(xn, shift, scale):
    return xn * (1 + scale) + shift


def _l2norm(x):
    xf = x.astype(jnp.float32)
    return xf * lax.rsqrt(jnp.sum(xf * xf, axis=-1, keepdims=True) + EPS)


def _rev(t):
    return jnp.flip(t, axis=2)


def _short_conv(x, w, n_rows):
    B, L, C = x.shape
    K = w.shape[0]
    xr = x.reshape(B * n_rows, L // n_rows, C)
    y = lax.conv_general_dilated(
        xr, w.astype(x.dtype)[:, None, :], window_strides=(1,),
        padding=[((K - 1) // 2, (K - 1) // 2)],
        dimension_numbers=('NWC', 'WIO', 'NWC'), feature_group_count=C)
    return y.reshape(B, L, C)


def _dn_inputs(hn, n_rows, w_in, conv_w, a_log, dt_bias):
    B, L, _ = hn.shape
    z = hn @ w_in
    qkv = jax.nn.silu(_short_conv(z[..., :3 * DN_INNER], conv_w, n_rows))
    gate = z[..., 3 * DN_INNER:4 * DN_INNER]
    ba = z[..., 4 * DN_INNER:].astype(jnp.float32).reshape(B, L, 2, 2, DN_HEADS)
    qkv = qkv.reshape(B, L, 3, DN_HEADS, DN_HEAD_DIM).transpose(2, 0, 3, 1, 4)
    q = _l2norm(qkv[0]) * (DN_HEAD_DIM ** -0.5)
    k = _l2norm(qkv[1])
    v = qkv[2].astype(jnp.float32)
    beta = jax.nn.sigmoid(ba[:, :, 0]).transpose(0, 2, 3, 1)
    g = (-jnp.exp(a_log.astype(jnp.float32)) *
         jax.nn.softplus(ba[:, :, 1] + dt_bias.astype(jnp.float32))).transpose(0, 2, 3, 1)
    return q, k, v, beta, g, gate


def _unit_lower_inverse(a):
    eye = jnp.eye(a.shape[-1], dtype=a.dtype)
    t = eye - a
    p = a
    for _ in range(DN_CHUNK_LOG2 - 1):
        p = p @ p
        t = t @ (eye + p)
    return t


def _gated_delta_chunked(k, v, beta, g, s0, q=None):
    f32 = jnp.float32
    with_out = q is not None
    B, H, L, DK = k.shape
    DV = v.shape[-1]
    C = DN_CHUNK
    n = L // C
    k = k.astype(f32).reshape(B, H, n, C, DK)
    v = v.astype(f32).reshape(B, H, n, C, DV)
    beta = beta.astype(f32).reshape(B, H, n, C)
    G = jnp.cumsum(g.astype(f32).reshape(B, H, n, C), axis=-1)
    incl = jnp.tril(jnp.ones((C, C), bool))
    strict = jnp.tril(jnp.ones((C, C), bool), -1)
    dmat = jnp.exp(jnp.where(incl, G[..., :, None] - G[..., None, :], -jnp.inf))
    kb = k * beta[..., None]
    a = jnp.where(strict, jnp.einsum('bhnid,bhnjd->bhnij', kb, k) * dmat, 0.0)
    t = _unit_lower_inverse(a)
    u = t @ (v * beta[..., None])
    w = t @ (kb * jnp.exp(G)[..., None])
    kd = k * jnp.exp(G[..., -1:] - G)[..., None]
    gl = jnp.exp(G[..., -1])
    xs = (jnp.moveaxis(w, 2, 0), jnp.moveaxis(u, 2, 0), jnp.moveaxis(kd, 2, 0), jnp.moveaxis(gl, 2, 0))
    if with_out:
        q = q.astype(f32).reshape(B, H, n, C, DK)
        qa = jnp.einsum('bhnid,bhnjd->bhnij', q, k) * dmat
        qg = q * jnp.exp(G)[..., None]
        xs = xs + (jnp.moveaxis(qg, 2, 0), jnp.moveaxis(qa, 2, 0))

    def step(s, inp):
        w_c, u_c, kd_c, gl_c = inp[:4]
        v_new = u_c - jnp.einsum('bhck,bhkv->bhcv', w_c, s)
        s_next = s * gl_c[..., None, None] + jnp.einsum('bhck,bhcv->bhkv', kd_c, v_new)
        if not with_out:
            return s_next, None
        qg_c, qa_c = inp[4:]
        o_c = jnp.einsum('bhck,bhkv->bhcv', qg_c, s) + jnp.einsum('bhij,bhjv->bhiv', qa_c, v_new)
        return s_next, o_c

    s_fin, o = lax.scan(step, s0.astype(f32), xs)
    if not with_out:
        return None, s_fin
    return jnp.moveaxis(o, 0, 2).reshape(B, H, L, DV), s_fin


def _gated_out(o, gate, g_norm, w_out):
    B, H, L, DV = o.shape
    o = jnp.swapaxes(o, 1, 2)
    on = o * lax.rsqrt(jnp.mean(o * o, axis=-1, keepdims=True) + EPS) * g_norm.astype(jnp.float32)
    y = on * jax.nn.silu(gate.astype(jnp.float32)).reshape(B, L, H, DV)
    return y.reshape(B, L, H * DV).astype(gate.dtype) @ w_out


def _deltanet(hn, hcn, n_rows, w_in, conv_w, a_log, dt_bias, o_norm_g, w_out, ctx_out):
    q, k, v, beta, g, gate = _dn_inputs(hn, n_rows, w_in, conv_w, a_log, dt_bias)
    qc, kc, vc, betac, gc, gatec = _dn_inputs(hcn, 1, w_in, conv_w, a_log, dt_bias)
    B = hn.shape[0]
    s0 = jnp.zeros((B, DN_HEADS, DN_HEAD_DIM, DN_HEAD_DIM), jnp.float32)
    oc_f, sc_f = _gated_delta_chunked(kc, vc, betac[:, 0], gc[:, 0], s0, qc if ctx_out else None)
    oc_b, sc_b = _gated_delta_chunked(_rev(kc), _rev(vc), _rev(betac[:, 1]), _rev(gc[:, 1]), s0,
                                      _rev(qc) if ctx_out else None)
    o_f, _ = _gated_delta_chunked(k, v, beta[:, 0], g[:, 0], sc_f, q)
    o_b, _ = _gated_delta_chunked(_rev(k), _rev(v), _rev(beta[:, 1]), _rev(g[:, 1]), sc_b, _rev(q))
    y = _gated_out(o_f + _rev(o_b), gate, o_norm_g, w_out)
    yc = _gated_out(oc_f + _rev(oc_b), gatec, o_norm_g, w_out) if ctx_out else None
    return y, yc


LANES = 128
PROJ_TOKENS_PER_STEP = 256
BF16 = jnp.bfloat16


def _norm_modulate(h_ref, g_ref, sh_ref, sc_ref):
    x = h_ref[...]
    xn = x * lax.rsqrt(jnp.mean(x * x, axis=-1, keepdims=True) + EPS) * g_ref[...]
    return xn * (1 + sc_ref[0]) + sh_ref[0]


def _row_conv(z, w, pos, row_len):
    n, taps = z.shape[0], w.shape[0]
    acc = None
    for tap in range(taps):
        off = tap - (taps - 1) // 2
        zs = z if off == 0 else pltpu.roll(z, (-off) % n, axis=0)
        ok = (pos + off >= 0) & (pos + off < row_len)
        term = jnp.where(ok, zs, 0.0) * w[tap:tap + 1]
        acc = term if acc is None else acc + term
    return acc


def _chunk_scan(x, pos, reverse):
    n = x.shape[0]
    s = 1
    while s < DN_CHUNK:
        if reverse:
            x = x + jnp.where(pos < DN_CHUNK - s, pltpu.roll(x, n - s, axis=0), 0.0)
        else:
            x = x + jnp.where(pos >= s, pltpu.roll(x, s, axis=0), 0.0)
        s *= 2
    return x


def _dn_inproj_kernel(h_ref, g_ref, sh_ref, sc_ref, w_ref, wb_ref, wa_ref, conv_ref, nea_ref, dtb_ref,
                      q_ref, k_ref, v_ref, gate_ref, beta_ref, pre_ref, suf_ref, tot_ref, *, row_len):
    xb = _norm_modulate(h_ref, g_ref, sh_ref, sc_ref).astype(BF16)
    tb = xb.shape[0]
    tok = lax.broadcasted_iota(jnp.int32, (tb, 1), 0)
    pos = tok % row_len
    for part, out_ref in enumerate((q_ref, k_ref, v_ref)):
        for head in range(DN_HEADS):
            c0 = part * DN_INNER + head * DN_HEAD_DIM
            z = jnp.dot(xb, w_ref[:, c0:c0 + DN_HEAD_DIM], preferred_element_type=jnp.float32)
            y = _row_conv(z, conv_ref[:, c0:c0 + DN_HEAD_DIM], pos, row_len)
            y = y * (1.0 / (1.0 + jnp.exp(-y)))
            if part < 2:
                y = y * lax.rsqrt(jnp.sum(y * y, axis=-1, keepdims=True) + EPS)
            if part == 0:
                y = y * (DN_HEAD_DIM ** -0.5)
            out_ref[:, head * DN_HEAD_DIM:(head + 1) * DN_HEAD_DIM] = y
    gate_ref[...] = jnp.dot(xb, w_ref[:, 3 * DN_INNER:], preferred_element_type=jnp.float32)
    zb = jnp.dot(xb, wb_ref[...], preferred_element_type=jnp.float32)
    za = jnp.dot(xb, wa_ref[...], preferred_element_type=jnp.float32) + dtb_ref[...]
    beta_ref[...] = 1.0 / (1.0 + jnp.exp(-zb))
    logdecay = nea_ref[...] * (jnp.maximum(za, 0.0) + jnp.log(1.0 + jnp.exp(-jnp.abs(za))))
    cpos = tok % DN_CHUNK
    pre = _chunk_scan(logdecay, cpos, False)
    suf = _chunk_scan(logdecay, cpos, True)
    pre_ref[...] = pre
    suf_ref[...] = suf
    tot_ref[...] = pre + suf - logdecay


def _dn_inproj(h, g, shift, scale, w_in, conv_w, a_log, dt_bias, row_len):
    B, L, D = h.shape
    T = B * L
    tb = min(PROJ_TOKENS_PER_STEP, L)
    assert L % tb == 0 and tb % row_len == 0 and tb % DN_CHUNK == 0
    steps_per_batch = L // tb
    nh2 = 2 * DN_HEADS
    lane_pad = lambda a: jnp.pad(a, ((0, 0), (0, LANES - nh2)))
    w_main = w_in[:, :4 * DN_INNER].astype(BF16)
    w_beta = lane_pad(w_in[:, 4 * DN_INNER:4 * DN_INNER + nh2]).astype(BF16)
    w_a = lane_pad(w_in[:, 4 * DN_INNER + nh2:]).astype(BF16)
    nea = lane_pad((-jnp.exp(a_log.astype(jnp.float32))).reshape(1, nh2))
    dtb = lane_pad(dt_bias.astype(jnp.float32).reshape(1, nh2))
    tok_spec = lambda n: pl.BlockSpec((tb, n), lambda i: (i, 0))
    full = lambda a: pl.BlockSpec(a.shape, lambda i: (0,) * a.ndim)
    mod_spec = pl.BlockSpec((1, 1, D), lambda i: (i // steps_per_batch, 0, 0))
    g2 = g.reshape(1, D)
    outs = pl.pallas_call(
        functools.partial(_dn_inproj_kernel, row_len=row_len),
        grid=(T // tb,),
        in_specs=[tok_spec(D), full(g2), mod_spec, mod_spec, full(w_main), full(w_beta), full(w_a),
                  full(conv_w), full(nea), full(dtb)],
        out_specs=[tok_spec(DN_INNER)] * 4 + [tok_spec(LANES)] * 4,
        out_shape=[jax.ShapeDtypeStruct((T, DN_INNER), jnp.float32)] * 4
        + [jax.ShapeDtypeStruct((T, LANES), jnp.float32)] * 4,
        compiler_params=pltpu.CompilerParams(dimension_semantics=("arbitrary",),
                                             vmem_limit_bytes=48 * 1024 * 1024),
        name="dn_inproj",
    )(h.reshape(T, D), g2, shift, scale, w_main, w_beta, w_a, conv_w, nea, dtb)
    q, k, v, gate, beta, pre, suf, tot = outs
    H = DN_HEADS
    zero = jnp.zeros((T, H), jnp.float32)
    fwd, bwd = slice(0, H), slice(H, 2 * H)
    rows = jnp.stack([beta[:, fwd], beta[:, bwd], pre[:, fwd], suf[:, bwd], tot[:, fwd], tot[:, bwd], zero, zero],
                     axis=0)
    return q, k, v, gate, rows.transpose(2, 0, 1)


DN_GROUP = 2 * DN_CHUNK
DN_SCAN_TOKENS_PER_STEP = 1024
NEG_BIG = -1e30


def _dn_groups(ks, vs, qs, rowss, dirs, states):
    n = DN_GROUP
    each = lambda f, *cols: [f(*a) for a in zip(*cols)]
    dot = functools.partial(jnp.dot, preferred_element_type=jnp.float32)
    dot_t = lambda a, b: lax.dot_general(a, b, (((1,), (1,)), ((), ())), preferred_element_type=jnp.float32)
    bf = lambda x: x.astype(BF16)
    ii = lax.broadcasted_iota(jnp.int32, (n, n), 0)
    jj = lax.broadcasted_iota(jnp.int32, (n, n), 1)
    same = (ii // DN_CHUNK) == (jj // DN_CHUNK)
    eye = (ii == jj).astype(jnp.float32)
    incl = [same & ((jj >= ii) if d else (jj <= ii)) for d in dirs]
    strict = [same & ((jj > ii) if d else (jj < ii)) for d in dirs]
    pad = jnp.zeros((n - SUBLANES, n), jnp.float32)
    cols = each(lambda r: jnp.concatenate([r, pad], axis=0).T, rowss)
    beta_c = each(lambda c, d: c[:, d:d + 1], cols, dirs)
    g_c = each(lambda c, d: c[:, 2 + d:3 + d], cols, dirs)
    gtot_c = each(lambda c, d: c[:, 4 + d:5 + d], cols, dirs)
    g_r = each(lambda r, d: r[2 + d:3 + d, :], rowss, dirs)
    dmat = each(lambda m, gc, gr: jnp.exp(jnp.where(m, gc - gr, NEG_BIG)), incl, g_c, g_r)
    kb = each(lambda k, b: k * b, ks, beta_c)
    k16 = each(bf, ks)
    kk = each(lambda a, b: dot_t(bf(a), b), kb, k16)
    a = each(lambda m, x, dm: jnp.where(m, x * dm, 0.0), strict, kk, dmat)
    t = each(lambda x: eye - x, a)
    p = a
    for _ in range(DN_CHUNK_LOG2 - 1):
        p = each(lambda x: dot(bf(x), bf(x)), p)
        t = each(lambda x, y: dot(bf(x), bf(eye + y)), t, p)
    t16 = each(bf, t)
    eg = each(jnp.exp, g_c)
    u = each(lambda tm, v, b: dot(tm, bf(v * b)), t16, vs, beta_c)
    w = each(lambda tm, x, e: bf(dot(tm, bf(x * e))), t16, kb, eg)
    kd_t = each(lambda k, gt, gc: (k * jnp.exp(gt - gc)).T, ks, gtot_c, g_c)
    qa = each(lambda q, k, dm: bf(dot_t(bf(q), k) * dm), qs, k16, dmat)
    qg = each(lambda q, e: bf(q * e), qs, eg)
    lane = lax.broadcasted_iota(jnp.int32, (ks[0].shape[1], n), 1) // DN_CHUNK
    row = lax.broadcasted_iota(jnp.int32, (n, vs[0].shape[1]), 0) // DN_CHUNK
    v_new = [jnp.zeros_like(x) for x in u]
    outs = [[None, None] for _ in dirs]
    for step in range(2):
        cs = [(1 - step) if d else step for d in dirs]
        rs = [slice(c * DN_CHUNK, (c + 1) * DN_CHUNK) for c in cs]
        s16 = each(bf, states)
        vn = each(lambda x, y, s, r: x[r] - dot(y[r], s), u, w, s16, rs)
        v_new = each(lambda x, old, c: jnp.where(row == c, jnp.concatenate([x, x], axis=0), old), vn, v_new, cs)
        vn16 = each(bf, v_new)
        o = each(lambda x, y, s, z, r: dot(x[r], s) + dot(y[r], z), qg, qa, s16, vn16, rs)
        for x, (c, oc) in enumerate(zip(cs, o)):
            outs[x][c] = oc
        states = each(lambda s, gt, c, kt, z: s * jnp.exp(gt[c * DN_CHUNK:c * DN_CHUNK + 1, :])
                      + dot(bf(jnp.where(lane == c, kt, 0.0)), z), states, gtot_c, cs, kd_t, vn16)
    return states, [jnp.concatenate(o, axis=0) for o in outs]


def _dn_scan_kernel(kf_ref, vf_ref, qf_ref, rf_ref, kb_ref, vb_ref, qb_ref, rb_ref, s0_ref,
                    of_ref, ob_ref, sfin_ref, state):
    i = pl.program_id(2)
    tl = kf_ref.shape[0]
    n_groups = tl // DN_GROUP
    hp = state.shape[0]
    dk = DN_HEAD_DIM

    @pl.when(i == 0)
    def _():
        state[...] = s0_ref[0]

    def group(gi, carry):
        rf = pl.ds(pl.multiple_of(gi * DN_GROUP, DN_GROUP), DN_GROUP)
        rb = pl.ds(pl.multiple_of((n_groups - 1 - gi) * DN_GROUP, DN_GROUP), DN_GROUP)
        ks, vs, qs, rowss, dirs = [], [], [], [], []
        for hh in range(hp):
            c = slice(hh * dk, (hh + 1) * dk)
            ks += [kf_ref[rf, c], kb_ref[rb, c]]
            vs += [vf_ref[rf, c], vb_ref[rb, c]]
            qs += [qf_ref[rf, c], qb_ref[rb, c]]
            rowss += [rf_ref[hh, :, rf], rb_ref[hh, :, rb]]
            dirs += [0, 1]
        states, outs = _dn_groups(ks, vs, qs, rowss, dirs, list(carry))
        for hh in range(hp):
            c = slice(hh * dk, (hh + 1) * dk)
            of_ref[rf, c] = outs[2 * hh]
            ob_ref[rb, c] = outs[2 * hh + 1]
        return tuple(states)

    init = tuple(state[hh, d] for hh in range(hp) for d in range(2))
    final = lax.fori_loop(0, n_groups, group, init)
    for hh in range(hp):
        for d in range(2):
            state[hh, d] = final[2 * hh + d]

    @pl.when(i == pl.num_programs(2) - 1)
    def _():
        for hh in range(hp):
            for d in range(2):
                sfin_ref[0, hh, d] = final[2 * hh + d]


DN_HEADS_PER_STEP = 4


def _dn_scan(q, k, v, rows, s0, B, L):
    T = B * L
    H, dk, hp = DN_HEADS, DN_HEAD_DIM, DN_HEADS_PER_STEP
    tl = min(DN_SCAN_TOKENS_PER_STEP, L)
    assert L % tl == 0 and tl % DN_GROUP == 0 and H % hp == 0
    nb = L // tl
    fwd = lambda b, h, i: (b * nb + i, h)
    bwd = lambda b, h, i: (b * nb + nb - 1 - i, h)
    tok = lambda m: pl.BlockSpec((tl, hp * dk), m)
    row_f = pl.BlockSpec((hp, 8, tl), lambda b, h, i: (h, 0, b * nb + i))
    row_b = pl.BlockSpec((hp, 8, tl), lambda b, h, i: (h, 0, b * nb + nb - 1 - i))
    st = pl.BlockSpec((1, hp, 2, dk, dk), lambda b, h, i: (b, h, 0, 0, 0))
    return pl.pallas_call(
        _dn_scan_kernel,
        grid=(B, H // hp, nb),
        in_specs=[tok(fwd), tok(fwd), tok(fwd), row_f, tok(bwd), tok(bwd), tok(bwd), row_b, st],
        out_specs=[tok(fwd), tok(bwd), st],
        out_shape=[jax.ShapeDtypeStruct((T, DN_INNER), jnp.float32)] * 2
        + [jax.ShapeDtypeStruct((B, H, 2, dk, dk), jnp.float32)],
        scratch_shapes=[pltpu.VMEM((hp, 2, dk, dk), jnp.float32)],
        compiler_params=pltpu.CompilerParams(dimension_semantics=("arbitrary", "arbitrary", "arbitrary"),
                                             vmem_limit_bytes=48 * 1024 * 1024),
        name="dn_scan",
    )(k, v, q, rows, k, v, q, rows, s0)


def _dn_outproj_kernel(of_ref, ob_ref, gate_ref, gn_ref, w_ref, h_ref, rg_ref, o_ref):
    parts = []
    for head in range(DN_HEADS):
        c = slice(head * DN_HEAD_DIM, (head + 1) * DN_HEAD_DIM)
        o = of_ref[:, c] + ob_ref[:, c]
        on = o * lax.rsqrt(jnp.mean(o * o, axis=-1, keepdims=True) + EPS) * gn_ref[...]
        gt = gate_ref[:, c]
        parts.append((on * (gt * (1.0 / (1.0 + jnp.exp(-gt))))).astype(BF16))
    y = jnp.concatenate(parts, axis=1)
    o_ref[...] = h_ref[...] + rg_ref[0] * jnp.dot(y, w_ref[...], preferred_element_type=jnp.float32)


def _dn_outproj(o_f, o_b, gate, g_norm, w_out, h, res_gate):
    T, D = h.shape
    tb = PROJ_TOKENS_PER_STEP
    steps_per_batch = T // res_gate.shape[0] // tb
    tok = pl.BlockSpec((tb, D), lambda i: (i, 0))
    gn = g_norm.reshape(1, DN_HEAD_DIM).astype(jnp.float32)
    w16 = w_out.astype(BF16)
    return pl.pallas_call(
        _dn_outproj_kernel,
        grid=(T // tb,),
        in_specs=[tok, tok, tok, pl.BlockSpec(gn.shape, lambda i: (0, 0)), pl.BlockSpec(w16.shape, lambda i: (0, 0)),
                  tok, pl.BlockSpec((1, 1, D), lambda i: (i // steps_per_batch, 0, 0))],
        out_specs=tok,
        out_shape=jax.ShapeDtypeStruct((T, D), h.dtype),
        compiler_params=pltpu.CompilerParams(dimension_semantics=("arbitrary",)),
        name="dn_outproj",
    )(o_f, o_b, gate, gn, w16, h, res_gate)


def _deltanet_residual(h, hc, g, shift, scale, shift_c, scale_c, res_gate,
                       w_in, conv_w, a_log, dt_bias, o_norm_g, w_out):
    B, L, D = h.shape
    Lc = hc.shape[1]
    sh_c = jnp.broadcast_to(shift_c, (B, 1, D))
    sc_c = jnp.broadcast_to(scale_c, (B, 1, D))
    qc, kc, vc, _, rows_c = _dn_inproj(hc, g, sh_c, sc_c, w_in, conv_w, a_log, dt_bias, row_len=Lc)
    zeros = jnp.zeros((B, DN_HEADS, 2, DN_HEAD_DIM, DN_HEAD_DIM), jnp.float32)
    _, _, s_ctx = _dn_scan(qc, kc, vc, rows_c, zeros, B, Lc)
    q, k, v, gate, rows = _dn_inproj(h, g, shift, scale, w_in, conv_w, a_log, dt_bias, row_len=GRID_W)
    o_f, o_b, _ = _dn_scan(q, k, v, rows, s_ctx, B, L)
    return _dn_outproj(o_f, o_b, gate, o_norm_g, w_out, h.reshape(B * L, D), res_gate).reshape(B, L, D)


def _hy_inproj_kernel(h_ref, g_ref, sh_ref, sc_ref, w_ref, conv_ref, bias_ref, v_ref, x1_ref, x2_ref, *, row_len):
    xb = _norm_modulate(h_ref, g_ref, sh_ref, sc_ref).astype(BF16)
    pos = lax.broadcasted_iota(jnp.int32, (xb.shape[0], 1), 0) % row_len
    for part, out_ref in enumerate((v_ref, x1_ref, x2_ref)):
        for tile in range(HY_WIDTH // LANES):
            c0 = part * HY_WIDTH + tile * LANES
            z = jnp.dot(xb, w_ref[:, c0:c0 + LANES], preferred_element_type=jnp.float32)
            y = _row_conv(z, conv_ref[:, c0:c0 + LANES], pos, row_len) + bias_ref[:, c0:c0 + LANES]
            out_ref[:, tile * LANES:(tile + 1) * LANES] = y


def _hy_inproj(h, g, shift, scale, w_in, conv_w, conv_b, row_len):
    B, L, D = h.shape
    T = B * L
    tb = min(PROJ_TOKENS_PER_STEP, L)
    assert L % tb == 0 and tb % row_len == 0
    steps_per_batch = L // tb
    tok_spec = lambda n: pl.BlockSpec((tb, n), lambda i: (i, 0))
    full = lambda a: pl.BlockSpec(a.shape, lambda i: (0,) * a.ndim)
    mod_spec = pl.BlockSpec((1, 1, D), lambda i: (i // steps_per_batch, 0, 0))
    g2, w16, bias = g.reshape(1, D), w_in.astype(BF16), conv_b.reshape(1, -1)
    return pl.pallas_call(
        functools.partial(_hy_inproj_kernel, row_len=row_len),
        grid=(T // tb,),
        in_specs=[tok_spec(D), full(g2), mod_spec, mod_spec, full(w16), full(conv_w), full(bias)],
        out_specs=[tok_spec(HY_WIDTH)] * 3,
        out_shape=[jax.ShapeDtypeStruct((T, HY_WIDTH), jnp.float32)] * 3,
        compiler_params=pltpu.CompilerParams(dimension_semantics=("arbitrary",),
                                             vmem_limit_bytes=48 * 1024 * 1024),
        name="hy_inproj",
    )(h.reshape(T, D), g2, shift, scale, w16, conv_w, bias)


def _residual_proj_kernel(z_ref, w_ref, h_ref, rg_ref, o_ref):
    o_ref[...] = h_ref[...] + rg_ref[0] * jnp.dot(z_ref[...].astype(BF16), w_ref[...],
                                                   preferred_element_type=jnp.float32)


def _residual_proj(z, w_out, h, res_gate):
    T, D = h.shape
    tb = PROJ_TOKENS_PER_STEP
    steps_per_batch = T // res_gate.shape[0] // tb
    w16 = w_out.astype(BF16)
    return pl.pallas_call(
        _residual_proj_kernel,
        grid=(T // tb,),
        in_specs=[pl.BlockSpec((tb, z.shape[1]), lambda i: (i, 0)), pl.BlockSpec(w16.shape, lambda i: (0, 0)),
                  pl.BlockSpec((tb, D), lambda i: (i, 0)),
                  pl.BlockSpec((1, 1, D), lambda i: (i // steps_per_batch, 0, 0))],
        out_specs=pl.BlockSpec((tb, D), lambda i: (i, 0)),
        out_shape=jax.ShapeDtypeStruct((T, D), h.dtype),
        compiler_params=pltpu.CompilerParams(dimension_semantics=("arbitrary",)),
        name="residual_proj",
    )(z, w16, h, res_gate)


SUBLANES = 8
FFT_P = 128
FFT_RA, FFT_RB = 8, 16
assert FFT_RA * FFT_RB == FFT_P
FFT_COLS_PER_STEP = 4


def _cmul_const(v, w):
    re, im = v
    wr, wi = round(w.real, 15), round(w.imag, 15)
    if wi == 0.0:
        return (re, im) if wr == 1.0 else ((-re, -im) if wr == -1.0 else (re * wr, im * wr))
    if wr == 0.0:
        return (-im, re) if wi == 1.0 else ((im, -re) if wi == -1.0 else (-im * wi, re * wi))
    return re * wr - im * wi, re * wi + im * wr


def _cmul(v, w):
    return v[0] * w[0] - v[1] * w[1], v[0] * w[1] + v[1] * w[0]


def _small_dft(xs, sign):
    n = len(xs)
    if n == 1:
        return xs
    even, odd = _small_dft(xs[0::2], sign), _small_dft(xs[1::2], sign)
    out = [None] * n
    for k in range(n // 2):
        t = _cmul_const(odd[k], cmath.exp(sign * 2j * math.pi * k / n))
        out[k] = (even[k][0] + t[0], even[k][1] + t[1])
        out[k + n // 2] = (even[k][0] - t[0], even[k][1] - t[1])
    return out


def _dft_p(load, store, tmp_re, tmp_im, inverse):
    sign = 1 if inverse else -1
    first, second = (FFT_RB, FFT_RA) if inverse else (FFT_RA, FFT_RB)
    for o in range(second):
        src = [load(FFT_RB * o + i) for i in range(first)] if inverse else [load(FFT_RB * i + o) for i in range(first)]
        for i, y in enumerate(_small_dft(src, sign)):
            b, c = (i, o) if inverse else (o, i)
            y = _cmul_const(y, cmath.exp(sign * 2j * math.pi * b * c / FFT_P))
            tmp_re[FFT_RB * c + b] = y[0]
            tmp_im[FFT_RB * c + b] = y[1]
    for o in range(first):
        idx = [FFT_RB * i + o for i in range(second)] if inverse else [FFT_RB * o + i for i in range(second)]
        for i, y in enumerate(_small_dft([(tmp_re[p], tmp_im[p]) for p in idx], sign)):
            store(FFT_RB * i + o if inverse else FFT_RB * o + i, y)


def _fft_cols_kernel(tw_ref, xr_ref, xi_ref, yr_ref, yi_ref, tmp_re, tmp_im, *, inverse):
    cols = yr_ref.shape[-3]
    col0 = pl.program_id(1) * cols

    def column(j, carry):
        if inverse:
            load = lambda n: (xr_ref[0, n, j], xi_ref[0, n, j])

            def store(n, y):
                if n < FFT_P // 2:
                    yr_ref[0, 0, n, j] = y[0]
                    yi_ref[0, 0, n, j] = y[1]
        else:
            load = lambda n: (xr_ref[0, 0, n, j], xi_ref[0, 0, n, j])

            def store(p, y):
                y = _cmul(y, (tw_ref[0, p, col0 + j], tw_ref[1, p, col0 + j]))
                yr_ref[0, p, j] = y[0]
                yi_ref[0, p, j] = y[1]
        _dft_p(load, store, tmp_re, tmp_im, inverse)
        return carry

    lax.fori_loop(0, cols, column, 0)


def _fft_rows_kernel(tw_ref, xr_ref, xi_ref, hr_ref, hi_ref, yr_ref, yi_ref, tmp_re, tmp_im, mid_re, mid_im, *,
                     convolve):
    rows = yr_ref.shape[1]
    row0 = pl.program_id(1) * rows

    def row(r, carry):
        load = lambda n: (xr_ref[0, r, n], xi_ref[0, r, n])
        if not convolve:
            def store(q, y):
                yr_ref[0, r, q] = y[0]
                yi_ref[0, r, q] = y[1]
            _dft_p(load, store, tmp_re, tmp_im, False)
            return carry

        def store_mid(q, y):
            y = _cmul(y, (hr_ref[0, r, q], hi_ref[0, r, q]))
            mid_re[q] = y[0]
            mid_im[q] = y[1]
        _dft_p(load, store_mid, tmp_re, tmp_im, False)

        def store(n, y):
            y = _cmul(y, (tw_ref[0, row0 + r, n], -tw_ref[1, row0 + r, n]))
            yr_ref[0, r, n] = y[0]
            yi_ref[0, r, n] = y[1]
        _dft_p(lambda q: (mid_re[q], mid_im[q]), store, tmp_re, tmp_im, True)
        return carry

    lax.fori_loop(0, rows, row, 0)


def _fft_twiddles():
    p = jnp.arange(FFT_P)
    k1 = p // FFT_RB + FFT_RA * (p % FFT_RB)
    ang = (2 * math.pi / (FFT_P * FFT_P)) * (k1[:, None] * jnp.arange(FFT_P)[None, :]).astype(jnp.float32)
    return jnp.stack([jnp.cos(ang), -jnp.sin(ang)]).astype(jnp.float32)


_TILE = (SUBLANES, LANES)
_FFT_PARAMS = dict(dimension_semantics=("arbitrary", "arbitrary"), vmem_limit_bytes=56 * 1024 * 1024)
_TMP = pltpu.VMEM((FFT_P,) + _TILE, jnp.float32)


def _fft_forward_cols(x, tw):
    G = x.shape[1]
    j = FFT_COLS_PER_STEP
    in_spec = lambda ri: pl.BlockSpec((1, 1, FFT_P, j) + _TILE, lambda g, c: (ri, g, 0, c, 0, 0))
    out_spec = pl.BlockSpec((1, FFT_P, j) + _TILE, lambda g, c: (g, 0, c, 0, 0))
    out = jax.ShapeDtypeStruct((G, FFT_P, FFT_P) + _TILE, jnp.float32)
    return pl.pallas_call(
        functools.partial(_fft_cols_kernel, inverse=False),
        grid=(G, FFT_P // j),
        in_specs=[pl.BlockSpec(memory_space=pltpu.SMEM), in_spec(0), in_spec(1)],
        out_specs=[out_spec, out_spec], out_shape=[out, out], scratch_shapes=[_TMP, _TMP],
        compiler_params=pltpu.CompilerParams(**_FFT_PARAMS), name="fft_fwd_cols",
    )(tw, x, x)


def _fft_rows(yr, yi, tw, hr=None, hi=None):
    G = yr.shape[0]
    k = FFT_COLS_PER_STEP
    spec = pl.BlockSpec((1, k, FFT_P) + _TILE, lambda g, r: (g, r, 0, 0, 0))
    out = jax.ShapeDtypeStruct(yr.shape, jnp.float32)
    convolve = hr is not None
    if not convolve:
        hr, hi = yr, yi
    return pl.pallas_call(
        functools.partial(_fft_rows_kernel, convolve=convolve),
        grid=(G, FFT_P // k),
        in_specs=[pl.BlockSpec(memory_space=pltpu.SMEM), spec, spec, spec, spec],
        out_specs=[spec, spec], out_shape=[out, out], scratch_shapes=[_TMP] * 4,
        compiler_params=pltpu.CompilerParams(**_FFT_PARAMS), name="fft_rows",
    )(tw, yr, yi, hr, hi)


def _fft_inverse_cols(yr, yi, tw):
    G = yr.shape[0]
    j = FFT_COLS_PER_STEP
    in_spec = pl.BlockSpec((1, FFT_P, j) + _TILE, lambda g, c: (g, 0, c, 0, 0))
    part = jax.ShapeDtypeStruct((1, G, FFT_P // 2, FFT_P) + _TILE, jnp.float32)
    o_spec = pl.BlockSpec((1, 1, FFT_P // 2, j) + _TILE, lambda g, c: (0, g, 0, c, 0, 0))
    re, im = pl.pallas_call(
        functools.partial(_fft_cols_kernel, inverse=True),
        grid=(G, FFT_P // j),
        in_specs=[pl.BlockSpec(memory_space=pltpu.SMEM), in_spec, in_spec],
        out_specs=[o_spec, o_spec], out_shape=[part, part], scratch_shapes=[_TMP, _TMP],
        compiler_params=pltpu.CompilerParams(**_FFT_PARAMS), name="fft_inv_cols",
    )(tw, yr, yi)
    return jnp.concatenate([re, im], axis=0)


def _to_fft_layout(u):
    B, L, C = u.shape
    half, ch = B // 2, SUBLANES // (B // 2)
    G = C // (ch * LANES)
    assert 2 * L == FFT_P * FFT_P and half * ch == SUBLANES and G * ch * LANES == C
    t = u.reshape(2, half, L, G, ch, LANES).transpose(0, 3, 2, 1, 4, 5).reshape(2, G, L, SUBLANES, LANES)
    t = jnp.pad(t, ((0, 0), (0, 0), (0, L), (0, 0), (0, 0)))
    return t.reshape(2, G, FFT_P, FFT_P, SUBLANES, LANES)


def _from_fft_layout(t, B, L, C):
    half, ch = B // 2, SUBLANES // (B // 2)
    G = C // (ch * LANES)
    return t.reshape(2, G, L, half, ch, LANES).transpose(0, 3, 2, 1, 4, 5).reshape(B, L, C)


def _filter_spectra(filt, tw, B):
    _, _, L, C = filt.shape
    half, ch = B // 2, SUBLANES // (B // 2)
    G = C // (ch * LANES)
    n = 2 * L
    kern = jnp.concatenate([filt[:, 0], jnp.zeros((2, 1, C), jnp.float32), jnp.flip(filt[:, 1, 1:], axis=1)], axis=1)
    t = kern.reshape(2, n, G, ch, LANES).transpose(2, 1, 0, 3, 4)
    t = jnp.pad(t, ((0, 0), (0, 0), (0, half - 2), (0, 0), (0, 0))).reshape(G, n, SUBLANES, LANES)
    x = jnp.stack([t, jnp.zeros_like(t)]).reshape(2, G, FFT_P, FFT_P, SUBLANES, LANES)
    fr, fi = _fft_rows(*_fft_forward_cols(x, tw), tw)
    spectra = []
    for order in range(2):
        pick = lambda a: jnp.broadcast_to(
            a.reshape(G, FFT_P, FFT_P, half, ch, LANES)[:, :, :, order:order + 1] * (1.0 / n),
            (G, FFT_P, FFT_P, half, ch, LANES)).reshape(G, FFT_P, FFT_P, SUBLANES, LANES)
        spectra.append((pick(fr), pick(fi)))
    return spectra


def _long_conv_fft(u, spectrum, tw):
    B, L, C = u.shape
    yr, yi = _fft_forward_cols(_to_fft_layout(u), tw)
    yr, yi = _fft_rows(yr, yi, tw, *spectrum)
    return _from_fft_layout(_fft_inverse_cols(yr, yi, tw), B, L, C)


def _hyena_residual(h, g, shift, scale, res_gate, w_in, conv_w, conv_b, f_w1, f_b1, f_w2, f_b2, f_w3, f_b3, f_w4,
                    freq, skip, w_out):
    B, L, D = h.shape
    v, x1, x2 = [a.reshape(B, L, HY_WIDTH) for a in _hy_inproj(h, g, shift, scale, w_in, conv_w, conv_b, GRID_W)]
    filt = _hyena_filters(L, f_w1, f_b1, f_w2, f_b2, f_w3, f_b3, f_w4, freq)
    tw = _fft_twiddles()
    spec1, spec2 = _filter_spectra(filt, tw, B)
    z1 = x1 * (_long_conv_fft(v, spec1, tw) + v * skip[0])
    z2 = x2 * (_long_conv_fft(z1, spec2, tw) + z1 * skip[1])
    return _residual_proj(z2.reshape(B * L, HY_WIDTH), w_out, h.reshape(B * L, D), res_gate).reshape(B, L, D)


def _hyena_filters(L, w1, b1, w2, b2, w3, b3, w4, freq):
    f32 = jnp.float32
    D = w4.shape[-1] // 4
    pos = jnp.arange(L, dtype=f32)
    t = pos / max(L - 1, 1)
    ang = (2 * math.pi * pos / L)[:, None] * jnp.linspace(1e-4, HY_BANDS - 1, HY_BANDS, dtype=f32)[None]
    z = jnp.concatenate([t[:, None], jnp.cos(ang), -jnp.sin(ang)], axis=-1)
    fr = freq.astype(f32)
    hid = jnp.sin(fr * (z @ w1.astype(f32) + b1.astype(f32)))
    hid = jnp.sin(fr * (hid @ w2.astype(f32) + b2.astype(f32)))
    hid = jnp.sin(fr * (hid @ w3.astype(f32) + b3.astype(f32)))
    h = hid @ w4.astype(f32)
    deltas = jnp.abs(jnp.linspace(math.log(HY_TARGET) / HY_FAST, math.log(HY_TARGET) / HY_SLOW, D, dtype=f32))
    decay = jnp.exp(-t[:, None] * deltas[None])
    return (h.reshape(L, 2, 2, D) * decay[:, None, None, :]).transpose(1, 2, 0, 3)


def _long_conv(u, hf, hb, d):
    B, L, D = u.shape
    filt2 = jnp.concatenate([hf, jnp.zeros((1, D), jnp.float32), jnp.flip(hb[1:], axis=0)], axis=0)
    uf = jnp.fft.rfft(u.astype(jnp.float32), n=2 * L, axis=1)
    y = jnp.fft.irfft(uf * jnp.fft.rfft(filt2, axis=0)[None], n=2 * L, axis=1)[:, :L]
    return (y + u.astype(jnp.float32) * d.astype(jnp.float32)).astype(u.dtype)


def _hyena(hn, n_rows, w_in, conv_w, conv_b, f_w1, f_b1, f_w2, f_b2, f_w3, f_b3, f_w4, freq, skip, w_out):
    B, L, _ = hn.shape
    z = _short_conv(hn @ w_in, conv_w, n_rows) + conv_b.astype(hn.dtype)
    v, x1, x2 = jnp.split(z, 3, axis=-1)
    filt = _hyena_filters(L, f_w1, f_b1, f_w2, f_b2, f_w3, f_b3, f_w4, freq)
    z1 = x1 * _long_conv(v, filt[0, 0], filt[0, 1], skip[0])
    z2 = x2 * _long_conv(z1, filt[1, 0], filt[1, 1], skip[1])
    return z2 @ w_out


PEER_SLOTS = PEER_HEADS * PEER_TOPK
PEER_TOKENS_PER_STEP = 64
PEER_GATHER_BUFFERS = 8


def _peer_expert_kernel(idx_ref, idx_next_ref, x_ref, gt_ref, h_ref, res_gate_ref, uv_ref, o_ref, buf, sem):
    tb, d = x_ref.shape
    nbuf = buf.shape[0]
    step = pl.program_id(0)
    last_step = pl.num_programs(0) - 1

    def row_copy(row, j, slot):
        return pltpu.make_async_copy(uv_ref.at[row], buf.at[slot, pl.ds(j, 1)], sem.at[slot])

    def start_token(ids_ref, t, slot):
        for j in range(PEER_SLOTS):
            row_copy(ids_ref[t, j], j, slot).start(priority=j % 2)

    def wait_token(slot):
        for j in range(PEER_SLOTS):
            row_copy(0, j, slot).wait()

    @pl.when(step == 0)
    def _():
        for t in range(nbuf - 1):
            start_token(idx_ref, t, t)

    lane = lax.broadcasted_iota(jnp.int32, (PEER_SLOTS, LANES), 1)
    lane0 = (step * tb) % LANES

    def finish_token(t, slot):
        wait_token(slot)
        x_row = x_ref[pl.ds(t, 1), :]
        hid = jnp.sum(buf[slot, :, :d] * x_row, axis=-1, keepdims=True)
        gate = jnp.sum(jnp.where(lane == lane0 + t, gt_ref[...], 0.0), axis=-1, keepdims=True)
        coef = 0.5 * hid * (1.0 + lax.erf(hid * (2.0 ** -0.5))) * gate
        y = jnp.sum(coef * buf[slot, :, d:], axis=0, keepdims=True)
        o_ref[pl.ds(t, 1), :] = h_ref[pl.ds(t, 1), :] + res_gate_ref[0] * y

    def token_group(g, carry):
        for slot in range(nbuf):
            t = g * nbuf + slot
            start_token(idx_ref, t + nbuf - 1, (slot + nbuf - 1) % nbuf)
            finish_token(t, slot)
        return carry

    lax.fori_loop(0, tb // nbuf - 1, token_group, 0)
    for slot in range(nbuf):
        t = tb - nbuf + slot
        ahead = t + nbuf - 1
        if ahead < tb:
            start_token(idx_ref, ahead, ahead % nbuf)
        else:
            start_token(idx_next_ref, ahead - tb, ahead % nbuf)
        finish_token(t, slot)

    @pl.when(step == last_step)
    def _():
        for slot in range(nbuf - 1):
            wait_token(slot)


def _peer_experts(xt, idx_t, gate_t, h, res_gate, u_tab, v_tab):
    T, D = xt.shape
    tb = PEER_TOKENS_PER_STEP
    nsteps = T // tb
    steps_per_batch = nsteps // res_gate.shape[0]
    assert T % LANES == 0 and LANES % tb == 0 and tb % PEER_GATHER_BUFFERS == 0
    uv = jnp.concatenate([u_tab, v_tab], axis=1)[:, None, :]
    steps_per_gate_tile = LANES // tb
    idx = idx_t.T
    return pl.pallas_call(
        _peer_expert_kernel,
        grid=(nsteps,),
        in_specs=[pl.BlockSpec((tb, PEER_SLOTS), lambda i: (i, 0), memory_space=pltpu.SMEM),
                  pl.BlockSpec((tb, PEER_SLOTS), lambda i: (jnp.minimum(i + 1, nsteps - 1), 0),
                               memory_space=pltpu.SMEM),
                  pl.BlockSpec((tb, D), lambda i: (i, 0)),
                  pl.BlockSpec((PEER_SLOTS, LANES), lambda i: (0, i // steps_per_gate_tile)),
                  pl.BlockSpec((tb, D), lambda i: (i, 0)),
                  pl.BlockSpec((1, 1, D), lambda i: (i // steps_per_batch, 0, 0)),
                  pl.BlockSpec(memory_space=pl.ANY)],
        out_specs=pl.BlockSpec((tb, D), lambda i: (i, 0)),
        out_shape=jax.ShapeDtypeStruct((T, D), xt.dtype),
        scratch_shapes=[pltpu.VMEM((PEER_GATHER_BUFFERS, PEER_SLOTS, 2 * D), jnp.float32),
                        pltpu.SemaphoreType.DMA((PEER_GATHER_BUFFERS,))],
        compiler_params=pltpu.CompilerParams(dimension_semantics=("arbitrary",)),
        name="peer_experts",
    )(idx, idx, xt, gate_t, h, res_gate, uv)


ROUTE_TOKENS_PER_STEP = 256
assert PEER_TOPK == 16


def _top_rows(s, order, payload, n):
    vals, outs = [], []
    for _ in range(n):
        m = jnp.max(s, axis=0, keepdims=True)
        first = jnp.min(jnp.where(s == m, order, jnp.inf), axis=0, keepdims=True)
        hit = order == first
        vals.append(m)
        outs.append(first if payload is None else jnp.max(jnp.where(hit, payload, -1.0), axis=0, keepdims=True))
        s = jnp.where(hit, -jnp.inf, s)
    return jnp.concatenate(vals, axis=0), jnp.concatenate(outs, axis=0)


def _candidate_blocks(s1, i1, s2, i2):
    k = PEER_TOPK
    c = s1.shape[1]
    r8 = lax.broadcasted_iota(jnp.int32, (8, c), 0).astype(jnp.float32)
    r16 = lax.broadcasted_iota(jnp.int32, (k, c), 0).astype(jnp.float32)
    ninf = -jnp.inf

    def col(i, rows, r, keep):
        return (jnp.where(keep, s1[i:i + 1] + s2[:rows], ninf), i * k + r, i1[i:i + 1] * PEER_NKEYS + i2[:rows])

    def row(j, rows, r, keep):
        return (jnp.where(keep, s1[:rows] + s2[j:j + 1], ninf), r * k + j, i1[:rows] * PEER_NKEYS + i2[j:j + 1])

    blocks = [row(0, k, r16, r16 >= 0), col(0, k, r16, r16 >= 1), col(1, 8, r8, r8 >= 1), row(1, 8, r8, r8 >= 2),
              row(2, 8, r8, (r8 >= 2) & (r8 <= 4)), row(3, 8, r8, (r8 >= 2) & (r8 <= 3)), row(4, 8, r8, r8 == 2)]
    return [jnp.concatenate([b[n] for b in blocks], axis=0) for n in range(3)]


def _peer_route_kernel(h_ref, g_ref, sh_ref, sc_ref, wq_ref, keys_ref, xn_ref, idx_ref, gate_ref):
    x = h_ref[...]
    xn = x * lax.rsqrt(jnp.mean(x * x, axis=-1, keepdims=True) + EPS) * g_ref[...]
    xm = xn * (1 + sc_ref[0]) + sh_ref[0]
    xn_ref[...] = xm
    xb = xm.astype(jnp.bfloat16)
    tb = x.shape[0]
    dk2 = PEER_DK // 2
    key_order = lax.broadcasted_iota(jnp.int32, (PEER_NKEYS, tb), 0).astype(jnp.float32)

    def head(h, carry):
        q = jnp.dot(xb, wq_ref[h], preferred_element_type=jnp.float32)
        tops = []
        for p in range(2):
            qp = q[:, p * dk2:(p + 1) * dk2].astype(jnp.bfloat16)
            s = lax.dot_general(keys_ref[h, p], qp, (((1,), (1,)), ((), ())),
                                preferred_element_type=jnp.float32)
            tops.append(_top_rows(s, key_order, None, PEER_TOPK))
        (s1, i1), (s2, i2) = tops
        cand, order, expert = _candidate_blocks(s1, i1, s2, i2)
        top_s, top_e = _top_rows(cand, order, expert, PEER_TOPK)
        e = jnp.exp(top_s - top_s[0:1])
        rows = pl.ds(pl.multiple_of(h * PEER_TOPK, PEER_TOPK), PEER_TOPK)
        gate_ref[rows, :] = e / jnp.sum(e, axis=0, keepdims=True)
        idx_ref[rows, :] = top_e.astype(jnp.int32)
        return carry

    lax.fori_loop(0, PEER_HEADS, head, 0)


def _peer_route(h, g, shift, scale, w_q, sub_keys):
    B, L, D = h.shape
    T = B * L
    tb = min(ROUTE_TOKENS_PER_STEP, L)
    assert L % tb == 0 and tb % LANES == 0
    steps_per_batch = L // tb
    dk2 = PEER_DK // 2
    wq = w_q.reshape(D, PEER_HEADS, 2 * dk2).transpose(1, 0, 2).astype(jnp.bfloat16)
    keys = sub_keys.astype(jnp.bfloat16)
    return pl.pallas_call(
        _peer_route_kernel,
        grid=(T // tb,),
        in_specs=[pl.BlockSpec((tb, D), lambda i: (i, 0)),
                  pl.BlockSpec((1, D), lambda i: (0, 0)),
                  pl.BlockSpec((1, 1, D), lambda i: (i // steps_per_batch, 0, 0)),
                  pl.BlockSpec((1, 1, D), lambda i: (i // steps_per_batch, 0, 0)),
                  pl.BlockSpec((PEER_HEADS, D, 2 * dk2), lambda i: (0, 0, 0)),
                  pl.BlockSpec((PEER_HEADS, 2, PEER_NKEYS, dk2), lambda i: (0, 0, 0, 0))],
        out_specs=[pl.BlockSpec((tb, D), lambda i: (i, 0)),
                   pl.BlockSpec((PEER_SLOTS, tb), lambda i: (0, i)),
                   pl.BlockSpec((PEER_SLOTS, tb), lambda i: (0, i))],
        out_shape=[jax.ShapeDtypeStruct((T, D), h.dtype),
                   jax.ShapeDtypeStruct((PEER_SLOTS, T), jnp.int32),
                   jax.ShapeDtypeStruct((PEER_SLOTS, T), jnp.float32)],
        compiler_params=pltpu.CompilerParams(dimension_semantics=("arbitrary",),
                                             vmem_limit_bytes=48 * 1024 * 1024),
        name="peer_route",
    )(h.reshape(T, D), g.reshape(1, D), shift, scale, wq, keys)


def _peer_residual(h, g, shift, scale, res_gate, w_q, sub_keys, u_tab, v_tab):
    B, L, D = h.shape
    xn, idx_t, gate_t = _peer_route(h, g, shift, scale, w_q, sub_keys)
    return _peer_experts(xn, idx_t, gate_t, h.reshape(B * L, D), res_gate, u_tab, v_tab).reshape(B, L, D)


def _final_rmsnorm_kernel(h_ref, g_ref, o_ref):
    x = h_ref[...]
    y = x * lax.rsqrt(jnp.mean(x * x, axis=-1, keepdims=True) + EPS)
    o_ref[...] = y * g_ref[...]


def _final_rmsnorm(h, g):
    B, L, D = h.shape
    T = B * L
    tm = 1024
    out = pl.pallas_call(
        _final_rmsnorm_kernel,
        grid=(T // tm,),
        in_specs=[pl.BlockSpec((tm, D), lambda i: (i, 0)),
                  pl.BlockSpec((1, D), lambda i: (0, 0))],
        out_specs=pl.BlockSpec((tm, D), lambda i: (i, 0)),
        out_shape=jax.ShapeDtypeStruct((T, D), h.dtype),
        name="final_rmsnorm",
    )(h.reshape(T, D), g.reshape(1, D))
    return out.reshape(B, L, D)


def kernel(x, c, ctx, c_ctx, ada_w, ada_b, norm_g,
           dn_w_in, dn_conv_w, dn_a_log, dn_dt_bias, dn_norm_g, dn_w_out,
           hy_w_in, hy_conv_w, hy_conv_b, hy_f_w1, hy_f_b1, hy_f_w2, hy_f_b2,
           hy_f_w3, hy_f_b3, hy_f_w4, hy_freq, hy_skip, hy_w_out,
           peer_w_q, peer_keys, peer_u, peer_v, final_g):
    f32 = jnp.float32
    B, L, D = x.shape
    rows = L // GRID_W
    silu_c = jax.nn.silu(c.astype(f32))
    silu_cc = jax.nn.silu(c_ctx.astype(f32))
    h, hc = x, ctx
    for i in range(DEPTH):
        kind = i % N_MIXERS
        j = i // N_MIXERS
        assert not any(l % N_MIXERS == MIXER_DELTA for l in range(i + 1, DEPTH))
        w_ada = ada_w[i].astype(f32)
        b_ada = ada_b[i].astype(f32)
        mod = (silu_c @ w_ada + b_ada).astype(h.dtype).reshape(B, 6, 1, D)
        sh1, sc1, gt1, sh2, sc2, gt2 = [mod[:, m] for m in range(6)]
        if kind == MIXER_DELTA:
            modc = (silu_cc @ w_ada + b_ada).astype(hc.dtype).reshape(6, 1, 1, D)
            h = _deltanet_residual(h, hc, norm_g[i, 0], sh1, sc1, modc[0], modc[1], gt1, dn_w_in[j], dn_conv_w[j],
                                   dn_a_log[j], dn_dt_bias[j], dn_norm_g[j], dn_w_out[j])
        else:
            hy_args = (hy_w_in[j], hy_conv_w[j], hy_conv_b[j], hy_f_w1[j], hy_f_b1[j], hy_f_w2[j],
                       hy_f_b2[j], hy_f_w3[j], hy_f_b3[j], hy_f_w4[j], hy_freq[j], hy_skip[j], hy_w_out[j])
            h = _hyena_residual(h, norm_g[i, 0], sh1, sc1, gt1, *hy_args)
        peer_args = (peer_w_q[i], peer_keys[i], peer_u[i], peer_v[i])
        h = _peer_residual(h, norm_g[i, 1], sh2, sc2, gt2, *peer_args)
    return _final_rmsnorm(h, final_g)
```

```python
import cmath
import functools
import math
import jax
import jax.numpy as jnp
from jax import lax
from jax.experimental import pallas as pl
from jax.experimental.pallas import tpu as pltpu

D_MODEL = 1024
BATCH = 8
SEQ = 8192
DEPTH = 2

GRID_W = 64
CTX_LEN = 256

N_MIXERS = 2
MIXER_DELTA = 0
MIXER_HYENA = 1

EPS = 1e-6

DN_HEAD_DIM = 128
DN_HEADS = D_MODEL // DN_HEAD_DIM
DN_INNER = DN_HEADS * DN_HEAD_DIM
DN_CONV = 5
DN_CHUNK = 64
DN_CHUNK_LOG2 = 6
DN_PROJ = 4 * DN_INNER + 4 * DN_HEADS

HY_WIDTH = D_MODEL
HY_SHORT = 3
HY_BANDS = 16
HY_EMB = 1 + 2 * HY_BANDS
HY_FF = 64
HY_FAST = 0.3
HY_SLOW = 1.5
HY_TARGET = 1e-2

PEER_HEADS = 8
PEER_NKEYS = 128
PEER_EXPERTS = PEER_NKEYS * PEER_NKEYS
PEER_DK = 256
PEER_TOPK = 16


LANES = 128
PROJ_TOKENS_PER_STEP = 256
BF16 = jnp.bfloat16


def _norm_modulate(h_ref, g_ref, sh_ref, sc_ref):
    x = h_ref[...]
    xn = x * lax.rsqrt(jnp.mean(x * x, axis=-1, keepdims=True) + EPS) * g_ref[...]
    return xn * (1 + sc_ref[0]) + sh_ref[0]


def _row_conv(z, w, pos, row_len):
    n, taps = z.shape[0], w.shape[0]
    acc = None
    for tap in range(taps):
        off = tap - (taps - 1) // 2
        zs = z if off == 0 else pltpu.roll(z, (-off) % n, axis=0)
        ok = (pos + off >= 0) & (pos + off < row_len)
        term = jnp.where(ok, zs, 0.0) * w[tap:tap + 1]
        acc = term if acc is None else acc + term
    return acc


def _chunk_scan(x, pos, reverse):
    n = x.shape[0]
    s = 1
    while s < DN_CHUNK:
        if reverse:
            x = x + jnp.where(pos < DN_CHUNK - s, pltpu.roll(x, n - s, axis=0), 0.0)
        else:
            x = x + jnp.where(pos >= s, pltpu.roll(x, s, axis=0), 0.0)
        s *= 2
    return x


def _dn_inproj_kernel(h_ref, g_ref, sh_ref, sc_ref, w_ref, wb_ref, wa_ref, conv_ref, nea_ref, dtb_ref,
                      q_ref, k_ref, v_ref, gate_ref, beta_ref, pre_ref, suf_ref, tot_ref, *, row_len):
    xb = _norm_modulate(h_ref, g_ref, sh_ref, sc_ref).astype(BF16)
    tb = xb.shape[0]
    tok = lax.broadcasted_iota(jnp.int32, (tb, 1), 0)
    pos = tok % row_len
    for part, out_ref in enumerate((q_ref, k_ref, v_ref)):
        for head in range(DN_HEADS):
            c0 = part * DN_INNER + head * DN_HEAD_DIM
            z = jnp.dot(xb, w_ref[:, c0:c0 + DN_HEAD_DIM], preferred_element_type=jnp.float32)
            y = _row_conv(z, conv_ref[:, c0:c0 + DN_HEAD_DIM], pos, row_len)
            y = y * (1.0 / (1.0 + jnp.exp(-y)))
            if part < 2:
                y = y * lax.rsqrt(jnp.sum(y * y, axis=-1, keepdims=True) + EPS)
            if part == 0:
                y = y * (DN_HEAD_DIM ** -0.5)
            out_ref[:, head * DN_HEAD_DIM:(head + 1) * DN_HEAD_DIM] = y
    gate_ref[...] = jnp.dot(xb, w_ref[:, 3 * DN_INNER:], preferred_element_type=jnp.float32)
    zb = jnp.dot(xb, wb_ref[...], preferred_element_type=jnp.float32)
    za = jnp.dot(xb, wa_ref[...], preferred_element_type=jnp.float32) + dtb_ref[...]
    beta_ref[...] = 1.0 / (1.0 + jnp.exp(-zb))
    logdecay = nea_ref[...] * (jnp.maximum(za, 0.0) + jnp.log(1.0 + jnp.exp(-jnp.abs(za))))
    cpos = tok % DN_CHUNK
    pre = _chunk_scan(logdecay, cpos, False)
    suf = _chunk_scan(logdecay, cpos, True)
    pre_ref[...] = pre
    suf_ref[...] = suf
    tot_ref[...] = pre + suf - logdecay


def _dn_inproj(h, g, shift, scale, w_in, conv_w, a_log, dt_bias, row_len):
    B, L, D = h.shape
    T = B * L
    tb = min(PROJ_TOKENS_PER_STEP, L)
    assert L % tb == 0 and tb % row_len == 0 and tb % DN_CHUNK == 0
    steps_per_batch = L // tb
    nh2 = 2 * DN_HEADS
    lane_pad = lambda a: jnp.pad(a, ((0, 0), (0, LANES - nh2)))
    w_main = w_in[:, :4 * DN_INNER].astype(BF16)
    w_beta = lane_pad(w_in[:, 4 * DN_INNER:4 * DN_INNER + nh2]).astype(BF16)
    w_a = lane_pad(w_in[:, 4 * DN_INNER + nh2:]).astype(BF16)
    nea = lane_pad((-jnp.exp(a_log.astype(jnp.float32))).reshape(1, nh2))
    dtb = lane_pad(dt_bias.astype(jnp.float32).reshape(1, nh2))
    tok_spec = lambda n: pl.BlockSpec((tb, n), lambda i: (i, 0))
    full = lambda a: pl.BlockSpec(a.shape, lambda i: (0,) * a.ndim)
    mod_spec = pl.BlockSpec((1, 1, D), lambda i: (i // steps_per_batch, 0, 0))
    g2 = g.reshape(1, D)
    outs = pl.pallas_call(
        functools.partial(_dn_inproj_kernel, row_len=row_len),
        grid=(T // tb,),
        in_specs=[tok_spec(D), full(g2), mod_spec, mod_spec, full(w_main), full(w_beta), full(w_a),
                  full(conv_w), full(nea), full(dtb)],
        out_specs=[tok_spec(DN_INNER)] * 4 + [tok_spec(LANES)] * 4,
        out_shape=[jax.ShapeDtypeStruct((T, DN_INNER), jnp.float32)] * 4
        + [jax.ShapeDtypeStruct((T, LANES), jnp.float32)] * 4,
        compiler_params=pltpu.CompilerParams(dimension_semantics=("arbitrary",),
                                             vmem_limit_bytes=48 * 1024 * 1024),
        name="dn_inproj",
    )(h.reshape(T, D), g2, shift, scale, w_main, w_beta, w_a, conv_w, nea, dtb)
    q, k, v, gate, beta, pre, suf, tot = outs
    H = DN_HEADS
    zero = jnp.zeros((T, H), jnp.float32)
    fwd, bwd = slice(0, H), slice(H, 2 * H)
    rows = jnp.stack([beta[:, fwd], beta[:, bwd], pre[:, fwd], suf[:, bwd], tot[:, fwd], tot[:, bwd], zero, zero],
                     axis=0)
    return q, k, v, gate, rows.transpose(2, 0, 1)


DN_GROUP = 2 * DN_CHUNK
DN_SCAN_TOKENS_PER_STEP = 1024
NEG_BIG = -1e30


def _dn_groups(ks, vs, qs, rowss, dirs, states):
    n = DN_GROUP
    each = lambda f, *cols: [f(*a) for a in zip(*cols)]
    dot = functools.partial(jnp.dot, preferred_element_type=jnp.float32)
    dot_t = lambda a, b: lax.dot_general(a, b, (((1,), (1,)), ((), ())), preferred_element_type=jnp.float32)
    bf = lambda x: x.astype(BF16)
    ii = lax.broadcasted_iota(jnp.int32, (n, n), 0)
    jj = lax.broadcasted_iota(jnp.int32, (n, n), 1)
    same = (ii // DN_CHUNK) == (jj // DN_CHUNK)
    eye = (ii == jj).astype(jnp.float32)
    incl = [same & ((jj >= ii) if d else (jj <= ii)) for d in dirs]
    strict = [same & ((jj > ii) if d else (jj < ii)) for d in dirs]
    pad = jnp.zeros((n - SUBLANES, n), jnp.float32)
    cols = each(lambda r: jnp.concatenate([r, pad], axis=0).T, rowss)
    beta_c = each(lambda c, d: c[:, d:d + 1], cols, dirs)
    g_c = each(lambda c, d: c[:, 2 + d:3 + d], cols, dirs)
    gtot_c = each(lambda c, d: c[:, 4 + d:5 + d], cols, dirs)
    g_r = each(lambda r, d: r[2 + d:3 + d, :], rowss, dirs)
    dmat = each(lambda m, gc, gr: jnp.exp(jnp.where(m, gc - gr, NEG_BIG)), incl, g_c, g_r)
    kb = each(lambda k, b: k * b, ks, beta_c)
    k16 = each(bf, ks)
    kk = each(lambda a, b: dot_t(bf(a), b), kb, k16)
    a = each(lambda m, x, dm: jnp.where(m, x * dm, 0.0), strict, kk, dmat)
    t = each(lambda x: eye - x, a)
    p = a
    for _ in range(DN_CHUNK_LOG2 - 1):
        p = each(lambda x: dot(bf(x), bf(x)), p)
        t = each(lambda x, y: dot(bf(x), bf(eye + y)), t, p)
    t16 = each(bf, t)
    eg = each(jnp.exp, g_c)
    u = each(lambda tm, v, b: dot(tm, bf(v * b)), t16, vs, beta_c)
    w = each(lambda tm, x, e: bf(dot(tm, bf(x * e))), t16, kb, eg)
    kd_t = each(lambda k, gt, gc: (k * jnp.exp(gt - gc)).T, ks, gtot_c, g_c)
    qa = each(lambda q, k, dm: bf(dot_t(bf(q), k) * dm), qs, k16, dmat)
    qg = each(lambda q, e: bf(q * e), qs, eg)
    lane = lax.broadcasted_iota(jnp.int32, (ks[0].shape[1], n), 1) // DN_CHUNK
    row = lax.broadcasted_iota(jnp.int32, (n, vs[0].shape[1]), 0) // DN_CHUNK
    v_new = [jnp.zeros_like(x) for x in u]
    outs = [[None, None] for _ in dirs]
    for step in range(2):
        cs = [(1 - step) if d else step for d in dirs]
        rs = [slice(c * DN_CHUNK, (c + 1) * DN_CHUNK) for c in cs]
        s16 = each(bf, states)
        vn = each(lambda x, y, s, r: x[r] - dot(y[r], s), u, w, s16, rs)
        v_new = each(lambda x, old, c: jnp.where(row == c, jnp.concatenate([x, x], axis=0), old), vn, v_new, cs)
        vn16 = each(bf, v_new)
        o = each(lambda x, y, s, z, r: dot(x[r], s) + dot(y[r], z), qg, qa, s16, vn16, rs)
        for x, (c, oc) in enumerate(zip(cs, o)):
            outs[x][c] = oc
        states = each(lambda s, gt, c, kt, z: s * jnp.exp(gt[c * DN_CHUNK:c * DN_CHUNK + 1, :])
                      + dot(bf(jnp.where(lane == c, kt, 0.0)), z), states, gtot_c, cs, kd_t, vn16)
    return states, [jnp.concatenate(o, axis=0) for o in outs]


def _dn_scan_kernel(kf_ref, vf_ref, qf_ref, rf_ref, kb_ref, vb_ref, qb_ref, rb_ref, s0_ref,
                    of_ref, ob_ref, sfin_ref, state):
    i = pl.program_id(2)
    tl = kf_ref.shape[0]
    n_groups = tl // DN_GROUP
    hp = state.shape[0]
    dk = DN_HEAD_DIM

    @pl.when(i == 0)
    def _():
        state[...] = s0_ref[0]

    def group(gi, carry):
        rf = pl.ds(pl.multiple_of(gi * DN_GROUP, DN_GROUP), DN_GROUP)
        rb = pl.ds(pl.multiple_of((n_groups - 1 - gi) * DN_GROUP, DN_GROUP), DN_GROUP)
        ks, vs, qs, rowss, dirs = [], [], [], [], []
        for hh in range(hp):
            c = slice(hh * dk, (hh + 1) * dk)
            ks += [kf_ref[rf, c], kb_ref[rb, c]]
            vs += [vf_ref[rf, c], vb_ref[rb, c]]
            qs += [qf_ref[rf, c], qb_ref[rb, c]]
            rowss += [rf_ref[hh, :, rf], rb_ref[hh, :, rb]]
            dirs += [0, 1]
        states, outs = _dn_groups(ks, vs, qs, rowss, dirs, list(carry))
        for hh in range(hp):
            c = slice(hh * dk, (hh + 1) * dk)
            of_ref[rf, c] = outs[2 * hh]
            ob_ref[rb, c] = outs[2 * hh + 1]
        return tuple(states)

    init = tuple(state[hh, d] for hh in range(hp) for d in range(2))
    final = lax.fori_loop(0, n_groups, group, init)
    for hh in range(hp):
        for d in range(2):
            state[hh, d] = final[2 * hh + d]

    @pl.when(i == pl.num_programs(2) - 1)
    def _():
        for hh in range(hp):
            for d in range(2):
                sfin_ref[0, hh, d] = final[2 * hh + d]


DN_HEADS_PER_STEP = 4


def _dn_scan(q, k, v, rows, s0, B, L):
    T = B * L
    H, dk, hp = DN_HEADS, DN_HEAD_DIM, DN_HEADS_PER_STEP
    tl = min(DN_SCAN_TOKENS_PER_STEP, L)
    assert L % tl == 0 and tl % DN_GROUP == 0 and H % hp == 0
    nb = L // tl
    fwd = lambda b, h, i: (b * nb + i, h)
    bwd = lambda b, h, i: (b * nb + nb - 1 - i, h)
    tok = lambda m: pl.BlockSpec((tl, hp * dk), m)
    row_f = pl.BlockSpec((hp, 8, tl), lambda b, h, i: (h, 0, b * nb + i))
    row_b = pl.BlockSpec((hp, 8, tl), lambda b, h, i: (h, 0, b * nb + nb - 1 - i))
    st = pl.BlockSpec((1, hp, 2, dk, dk), lambda b, h, i: (b, h, 0, 0, 0))
    return pl.pallas_call(
        _dn_scan_kernel,
        grid=(B, H // hp, nb),
        in_specs=[tok(fwd), tok(fwd), tok(fwd), row_f, tok(bwd), tok(bwd), tok(bwd), row_b, st],
        out_specs=[tok(fwd), tok(bwd), st],
        out_shape=[jax.ShapeDtypeStruct((T, DN_INNER), jnp.float32)] * 2
        + [jax.ShapeDtypeStruct((B, H, 2, dk, dk), jnp.float32)],
        scratch_shapes=[pltpu.VMEM((hp, 2, dk, dk), jnp.float32)],
        compiler_params=pltpu.CompilerParams(dimension_semantics=("arbitrary", "arbitrary", "arbitrary"),
                                             vmem_limit_bytes=48 * 1024 * 1024),
        name="dn_scan",
    )(k, v, q, rows, k, v, q, rows, s0)


def _dn_outproj_kernel(of_ref, ob_ref, gate_ref, gn_ref, w_ref, h_ref, rg_ref, o_ref):
    parts = []
    for head in range(DN_HEADS):
        c = slice(head * DN_HEAD_DIM, (head + 1) * DN_HEAD_DIM)
        o = of_ref[:, c] + ob_ref[:, c]
        on = o * lax.rsqrt(jnp.mean(o * o, axis=-1, keepdims=True) + EPS) * gn_ref[...]
        gt = gate_ref[:, c]
        parts.append((on * (gt * (1.0 / (1.0 + jnp.exp(-gt))))).astype(BF16))
    y = jnp.concatenate(parts, axis=1)
    o_ref[...] = h_ref[...] + rg_ref[0] * jnp.dot(y, w_ref[...], preferred_element_type=jnp.float32)


def _dn_outproj(o_f, o_b, gate, g_norm, w_out, h, res_gate):
    T, D = h.shape
    tb = PROJ_TOKENS_PER_STEP
    steps_per_batch = T // res_gate.shape[0] // tb
    tok = pl.BlockSpec((tb, D), lambda i: (i, 0))
    gn = g_norm.reshape(1, DN_HEAD_DIM).astype(jnp.float32)
    w16 = w_out.astype(BF16)
    return pl.pallas_call(
        _dn_outproj_kernel,
        grid=(T // tb,),
        in_specs=[tok, tok, tok, pl.BlockSpec(gn.shape, lambda i: (0, 0)), pl.BlockSpec(w16.shape, lambda i: (0, 0)),
                  tok, pl.BlockSpec((1, 1, D), lambda i: (i // steps_per_batch, 0, 0))],
        out_specs=tok,
        out_shape=jax.ShapeDtypeStruct((T, D), h.dtype),
        compiler_params=pltpu.CompilerParams(dimension_semantics=("arbitrary",)),
        name="dn_outproj",
    )(o_f, o_b, gate, gn, w16, h, res_gate)


def _deltanet_residual(h, hc, g, shift, scale, shift_c, scale_c, res_gate,
                       w_in, conv_w, a_log, dt_bias, o_norm_g, w_out):
    B, L, D = h.shape
    Lc = hc.shape[1]
    sh_c = jnp.broadcast_to(shift_c, (B, 1, D))
    sc_c = jnp.broadcast_to(scale_c, (B, 1, D))
    qc, kc, vc, _, rows_c = _dn_inproj(hc, g, sh_c, sc_c, w_in, conv_w, a_log, dt_bias, row_len=Lc)
    zeros = jnp.zeros((B, DN_HEADS, 2, DN_HEAD_DIM, DN_HEAD_DIM), jnp.float32)
    _, _, s_ctx = _dn_scan(qc, kc, vc, rows_c, zeros, B, Lc)
    q, k, v, gate, rows = _dn_inproj(h, g, shift, scale, w_in, conv_w, a_log, dt_bias, row_len=GRID_W)
    o_f, o_b, _ = _dn_scan(q, k, v, rows, s_ctx, B, L)
    return _dn_outproj(o_f, o_b, gate, o_norm_g, w_out, h.reshape(B * L, D), res_gate).reshape(B, L, D)


def _hy_inproj_kernel(h_ref, g_ref, sh_ref, sc_ref, w_ref, conv_ref, bias_ref, v_ref, x1_ref, x2_ref, *, row_len):
    xb = _norm_modulate(h_ref, g_ref, sh_ref, sc_ref).astype(BF16)
    pos = lax.broadcasted_iota(jnp.int32, (xb.shape[0], 1), 0) % row_len
    for part, out_ref in enumerate((v_ref, x1_ref, x2_ref)):
        for tile in range(HY_WIDTH // LANES):
            c0 = part * HY_WIDTH + tile * LANES
            z = jnp.dot(xb, w_ref[:, c0:c0 + LANES], preferred_element_type=jnp.float32)
            y = _row_conv(z, conv_ref[:, c0:c0 + LANES], pos, row_len) + bias_ref[:, c0:c0 + LANES]
            out_ref[:, tile * LANES:(tile + 1) * LANES] = y


def _hy_inproj(h, g, shift, scale, w_in, conv_w, conv_b, row_len):
    B, L, D = h.shape
    T = B * L
    tb = min(PROJ_TOKENS_PER_STEP, L)
    assert L % tb == 0 and tb % row_len == 0
    steps_per_batch = L // tb
    tok_spec = lambda n: pl.BlockSpec((tb, n), lambda i: (i, 0))
    full = lambda a: pl.BlockSpec(a.shape, lambda i: (0,) * a.ndim)
    mod_spec = pl.BlockSpec((1, 1, D), lambda i: (i // steps_per_batch, 0, 0))
    g2, w16, bias = g.reshape(1, D), w_in.astype(BF16), conv_b.reshape(1, -1)
    return pl.pallas_call(
        functools.partial(_hy_inproj_kernel, row_len=row_len),
        grid=(T // tb,),
        in_specs=[tok_spec(D), full(g2), mod_spec, mod_spec, full(w16), full(conv_w), full(bias)],
        out_specs=[tok_spec(HY_WIDTH)] * 3,
        out_shape=[jax.ShapeDtypeStruct((T, HY_WIDTH), jnp.float32)] * 3,
        compiler_params=pltpu.CompilerParams(dimension_semantics=("arbitrary",),
                                             vmem_limit_bytes=48 * 1024 * 1024),
        name="hy_inproj",
    )(h.reshape(T, D), g2, shift, scale, w16, conv_w, bias)


def _residual_proj_kernel(z_ref, w_ref, h_ref, rg_ref, o_ref):
    o_ref[...] = h_ref[...] + rg_ref[0] * jnp.dot(z_ref[...].astype(BF16), w_ref[...],
                                                   preferred_element_type=jnp.float32)


def _residual_proj(z, w_out, h, res_gate):
    T, D = h.shape
    tb = PROJ_TOKENS_PER_STEP
    steps_per_batch = T // res_gate.shape[0] // tb
    w16 = w_out.astype(BF16)
    return pl.pallas_call(
        _residual_proj_kernel,
        grid=(T // tb,),
        in_specs=[pl.BlockSpec((tb, z.shape[1]), lambda i: (i, 0)), pl.BlockSpec(w16.shape, lambda i: (0, 0)),
                  pl.BlockSpec((tb, D), lambda i: (i, 0)),
                  pl.BlockSpec((1, 1, D), lambda i: (i // steps_per_batch, 0, 0))],
        out_specs=pl.BlockSpec((tb, D), lambda i: (i, 0)),
        out_shape=jax.ShapeDtypeStruct((T, D), h.dtype),
        compiler_params=pltpu.CompilerParams(dimension_semantics=("arbitrary",)),
        name="residual_proj",
    )(z, w16, h, res_gate)


SUBLANES = 8
FFT_P = 128
FFT_RA, FFT_RB = 8, 16
assert FFT_RA * FFT_RB == FFT_P
FFT_COLS_PER_STEP = 4


def _cmul_const(v, w):
    re, im = v
    wr, wi = round(w.real, 15), round(w.imag, 15)
    if wi == 0.0:
        return (re, im) if wr == 1.0 else ((-re, -im) if wr == -1.0 else (re * wr, im * wr))
    if wr == 0.0:
        return (-im, re) if wi == 1.0 else ((im, -re) if wi == -1.0 else (-im * wi, re * wi))
    return re * wr - im * wi, re * wi + im * wr


def _cmul(v, w):
    return v[0] * w[0] - v[1] * w[1], v[0] * w[1] + v[1] * w[0]


def _small_dft(xs, sign):
    n = len(xs)
    if n == 1:
        return xs
    even, odd = _small_dft(xs[0::2], sign), _small_dft(xs[1::2], sign)
    out = [None] * n
    for k in range(n // 2):
        e = even[k]
        t = None if odd[k] is None else _cmul_const(odd[k], cmath.exp(sign * 2j * math.pi * k / n))
        if t is None:
            out[k] = out[k + n // 2] = e
        elif e is None:
            out[k], out[k + n // 2] = t, (-t[0], -t[1])
        else:
            out[k] = (e[0] + t[0], e[1] + t[1])
            out[k + n // 2] = (e[0] - t[0], e[1] - t[1])
    return out


def _dft_p(load, store, tmp_re, tmp_im, inverse):
    sign = 1 if inverse else -1
    first, second = (FFT_RB, FFT_RA) if inverse else (FFT_RA, FFT_RB)
    for o in range(second):
        src = [load(FFT_RB * o + i) for i in range(first)] if inverse else [load(FFT_RB * i + o) for i in range(first)]
        for i, y in enumerate(_small_dft(src, sign)):
            b, c = (i, o) if inverse else (o, i)
            y = _cmul_const(y, cmath.exp(sign * 2j * math.pi * b * c / FFT_P))
            tmp_re[FFT_RB * c + b] = y[0]
            tmp_im[FFT_RB * c + b] = y[1]
    for o in range(first):
        idx = [FFT_RB * i + o for i in range(second)] if inverse else [FFT_RB * o + i for i in range(second)]
        for i, y in enumerate(_small_dft([(tmp_re[p], tmp_im[p]) for p in idx], sign)):
            store(FFT_RB * i + o if inverse else FFT_RB * o + i, y)


def _fft_cols_kernel(tw_ref, xr_ref, xi_ref, yr_ref, yi_ref, tmp_re, tmp_im, *, inverse):
    cols = yr_ref.shape[-3]
    col0 = pl.program_id(1) * cols

    def column(j, carry):
        if inverse:
            load = lambda n: (xr_ref[0, n, j], xi_ref[0, n, j])

            def store(n, y):
                if n < FFT_P // 2:
                    yr_ref[0, 0, n, j] = y[0]
                    yi_ref[0, 0, n, j] = y[1]
        else:
            n_in = xr_ref.shape[2]
            load = lambda n: (xr_ref[0, 0, n, j], xi_ref[0, 0, n, j]) if n < n_in else None

            def store(p, y):
                y = _cmul(y, (tw_ref[0, p, col0 + j], tw_ref[1, p, col0 + j]))
                yr_ref[0, p, j] = y[0]
                yi_ref[0, p, j] = y[1]
        _dft_p(load, store, tmp_re, tmp_im, inverse)
        return carry

    lax.fori_loop(0, cols, column, 0)


def _fft_rows_kernel(tw_ref, xr_ref, xi_ref, hr_ref, hi_ref, yr_ref, yi_ref, tmp_re, tmp_im, mid_re, mid_im, *,
                     convolve):
    rows = yr_ref.shape[1]
    row0 = pl.program_id(1) * rows

    def row(r, carry):
        load = lambda n: (xr_ref[0, r, n], xi_ref[0, r, n])
        if not convolve:
            def store(q, y):
                yr_ref[0, r, q] = y[0]
                yi_ref[0, r, q] = y[1]
            _dft_p(load, store, tmp_re, tmp_im, False)
            return carry

        def store_mid(q, y):
            y = _cmul(y, (hr_ref[0, r, q], hi_ref[0, r, q]))
            mid_re[q] = y[0]
            mid_im[q] = y[1]
        _dft_p(load, store_mid, tmp_re, tmp_im, False)

        def store(n, y):
            y = _cmul(y, (tw_ref[0, row0 + r, n], -tw_ref[1, row0 + r, n]))
            yr_ref[0, r, n] = y[0]
            yi_ref[0, r, n] = y[1]
        _dft_p(lambda q: (mid_re[q], mid_im[q]), store, tmp_re, tmp_im, True)
        return carry

    lax.fori_loop(0, rows, row, 0)


def _fft_twiddles():
    p = jnp.arange(FFT_P)
    k1 = p // FFT_RB + FFT_RA * (p % FFT_RB)
    ang = (2 * math.pi / (FFT_P * FFT_P)) * (k1[:, None] * jnp.arange(FFT_P)[None, :]).astype(jnp.float32)
    return jnp.stack([jnp.cos(ang), -jnp.sin(ang)]).astype(jnp.float32)


_TILE = (SUBLANES, LANES)
_FFT_PARAMS = dict(dimension_semantics=("arbitrary", "arbitrary"), vmem_limit_bytes=56 * 1024 * 1024)
_TMP = pltpu.VMEM((FFT_P,) + _TILE, jnp.float32)


def _fft_forward_cols(x, tw):
    G = x.shape[1]
    j = FFT_COLS_PER_STEP
    in_spec = lambda ri: pl.BlockSpec((1, 1, x.shape[2], j) + _TILE, lambda g, c: (ri, g, 0, c, 0, 0))
    out_spec = pl.BlockSpec((1, FFT_P, j) + _TILE, lambda g, c: (g, 0, c, 0, 0))
    out = jax.ShapeDtypeStruct((G, FFT_P, FFT_P) + _TILE, jnp.float32)
    return pl.pallas_call(
        functools.partial(_fft_cols_kernel, inverse=False),
        grid=(G, FFT_P // j),
        in_specs=[pl.BlockSpec(memory_space=pltpu.SMEM), in_spec(0), in_spec(1)],
        out_specs=[out_spec, out_spec], out_shape=[out, out], scratch_shapes=[_TMP, _TMP],
        compiler_params=pltpu.CompilerParams(**_FFT_PARAMS), name="fft_fwd_cols",
    )(tw, x, x)


def _fft_rows(yr, yi, tw, hr=None, hi=None):
    G = yr.shape[0]
    k = FFT_COLS_PER_STEP
    spec = pl.BlockSpec((1, k, FFT_P) + _TILE, lambda g, r: (g, r, 0, 0, 0))
    out = jax.ShapeDtypeStruct(yr.shape, jnp.float32)
    convolve = hr is not None
    if not convolve:
        hr, hi = yr, yi
    return pl.pallas_call(
        functools.partial(_fft_rows_kernel, convolve=convolve),
        grid=(G, FFT_P // k),
        in_specs=[pl.BlockSpec(memory_space=pltpu.SMEM), spec, spec, spec, spec],
        out_specs=[spec, spec], out_shape=[out, out], scratch_shapes=[_TMP] * 4,
        compiler_params=pltpu.CompilerParams(**_FFT_PARAMS), name="fft_rows",
    )(tw, yr, yi, hr, hi)


def _fft_inverse_cols(yr, yi, tw):
    G = yr.shape[0]
    j = FFT_COLS_PER_STEP
    in_spec = pl.BlockSpec((1, FFT_P, j) + _TILE, lambda g, c: (g, 0, c, 0, 0))
    part = jax.ShapeDtypeStruct((1, G, FFT_P // 2, FFT_P) + _TILE, jnp.float32)
    o_spec = pl.BlockSpec((1, 1, FFT_P // 2, j) + _TILE, lambda g, c: (0, g, 0, c, 0, 0))
    re, im = pl.pallas_call(
        functools.partial(_fft_cols_kernel, inverse=True),
        grid=(G, FFT_P // j),
        in_specs=[pl.BlockSpec(memory_space=pltpu.SMEM), in_spec, in_spec],
        out_specs=[o_spec, o_spec], out_shape=[part, part], scratch_shapes=[_TMP, _TMP],
        compiler_params=pltpu.CompilerParams(**_FFT_PARAMS), name="fft_inv_cols",
    )(tw, yr, yi)
    return jnp.concatenate([re, im], axis=0)


def _to_fft_layout(u):
    B, L, C = u.shape
    half, ch = B // 2, SUBLANES // (B // 2)
    G = C // (ch * LANES)
    assert 2 * L == FFT_P * FFT_P and half * ch == SUBLANES and G * ch * LANES == C
    t = u.reshape(2, half, L, G, ch, LANES).transpose(0, 3, 2, 1, 4, 5).reshape(2, G, L, SUBLANES, LANES)
    return t.reshape(2, G, FFT_P // 2, FFT_P, SUBLANES, LANES)


def _from_fft_layout(t, B, L, C):
    half, ch = B // 2, SUBLANES // (B // 2)
    G = C // (ch * LANES)
    return t.reshape(2, G, L, half, ch, LANES).transpose(0, 3, 2, 1, 4, 5).reshape(B, L, C)


def _filter_spectra(filt, tw, B):
    _, _, L, C = filt.shape
    half, ch = B // 2, SUBLANES // (B // 2)
    G = C // (ch * LANES)
    n = 2 * L
    kern = jnp.concatenate([filt[:, 0], jnp.zeros((2, 1, C), jnp.float32), jnp.flip(filt[:, 1, 1:], axis=1)], axis=1)
    t = kern.reshape(2, n, G, ch, LANES).transpose(2, 1, 0, 3, 4)
    t = jnp.pad(t, ((0, 0), (0, 0), (0, half - 2), (0, 0), (0, 0))).reshape(G, n, SUBLANES, LANES)
    x = jnp.stack([t, jnp.zeros_like(t)]).reshape(2, G, FFT_P, FFT_P, SUBLANES, LANES)
    fr, fi = _fft_rows(*_fft_forward_cols(x, tw), tw)
    spectra = []
    for order in range(2):
        pick = lambda a: jnp.broadcast_to(
            a.reshape(G, FFT_P, FFT_P, half, ch, LANES)[:, :, :, order:order + 1] * (1.0 / n),
            (G, FFT_P, FFT_P, half, ch, LANES)).reshape(G, FFT_P, FFT_P, SUBLANES, LANES)
        spectra.append((pick(fr), pick(fi)))
    return spectra


def _long_conv_fft(u, spectrum, tw):
    B, L, C = u.shape
    yr, yi = _fft_forward_cols(_to_fft_layout(u), tw)
    yr, yi = _fft_rows(yr, yi, tw, *spectrum)
    return _from_fft_layout(_fft_inverse_cols(yr, yi, tw), B, L, C)


def _hyena_residual(h, g, shift, scale, res_gate, w_in, conv_w, conv_b, f_w1, f_b1, f_w2, f_b2, f_w3, f_b3, f_w4,
                    freq, skip, w_out):
    B, L, D = h.shape
    v, x1, x2 = [a.reshape(B, L, HY_WIDTH) for a in _hy_inproj(h, g, shift, scale, w_in, conv_w, conv_b, GRID_W)]
    filt = _hyena_filters(L, f_w1, f_b1, f_w2, f_b2, f_w3, f_b3, f_w4, freq)
    tw = _fft_twiddles()
    spec1, spec2 = _filter_spectra(filt, tw, B)
    z1 = x1 * (_long_conv_fft(v, spec1, tw) + v * skip[0])
    z2 = x2 * (_long_conv_fft(z1, spec2, tw) + z1 * skip[1])
    return _residual_proj(z2.reshape(B * L, HY_WIDTH), w_out, h.reshape(B * L, D), res_gate).reshape(B, L, D)


def _hyena_filters(L, w1, b1, w2, b2, w3, b3, w4, freq):
    f32 = jnp.float32
    D = w4.shape[-1] // 4
    pos = jnp.arange(L, dtype=f32)
    t = pos / max(L - 1, 1)
    ang = (2 * math.pi * pos / L)[:, None] * jnp.linspace(1e-4, HY_BANDS - 1, HY_BANDS, dtype=f32)[None]
    z = jnp.concatenate([t[:, None], jnp.cos(ang), -jnp.sin(ang)], axis=-1)
    fr = freq.astype(f32)
    hid = jnp.sin(fr * (z @ w1.astype(f32) + b1.astype(f32)))
    hid = jnp.sin(fr * (hid @ w2.astype(f32) + b2.astype(f32)))
    hid = jnp.sin(fr * (hid @ w3.astype(f32) + b3.astype(f32)))
    h = hid @ w4.astype(f32)
    deltas = jnp.abs(jnp.linspace(math.log(HY_TARGET) / HY_FAST, math.log(HY_TARGET) / HY_SLOW, D, dtype=f32))
    decay = jnp.exp(-t[:, None] * deltas[None])
    return (h.reshape(L, 2, 2, D) * decay[:, None, None, :]).transpose(1, 2, 0, 3)


PEER_SLOTS = PEER_HEADS * PEER_TOPK
PEER_TOKENS_PER_STEP = 64
PEER_GATHER_BUFFERS = 8


def _peer_expert_kernel(idx_ref, idx_next_ref, x_ref, gt_ref, h_ref, res_gate_ref, uv_ref, o_ref, buf, sem):
    tb, d = x_ref.shape
    nbuf = buf.shape[0]
    step = pl.program_id(0)
    last_step = pl.num_programs(0) - 1

    def row_copy(row, j, slot):
        return pltpu.make_async_copy(uv_ref.at[row], buf.at[slot, pl.ds(j, 1)], sem.at[slot])

    def start_token(ids_ref, t, slot):
        for j in range(PEER_SLOTS):
            row_copy(ids_ref[t, j], j, slot).start(priority=j % 2)

    def wait_token(slot):
        for j in range(PEER_SLOTS):
            row_copy(0, j, slot).wait()

    @pl.when(step == 0)
    def _():
        for t in range(nbuf - 1):
            start_token(idx_ref, t, t)

    lane = lax.broadcasted_iota(jnp.int32, (PEER_SLOTS, LANES), 1)
    lane0 = (step * tb) % LANES

    def finish_token(t, slot):
        wait_token(slot)
        x_row = x_ref[pl.ds(t, 1), :]
        hid = jnp.sum(buf[slot, :, :d] * x_row, axis=-1, keepdims=True)
        gate = jnp.sum(jnp.where(lane == lane0 + t, gt_ref[...], 0.0), axis=-1, keepdims=True)
        coef = 0.5 * hid * (1.0 + lax.erf(hid * (2.0 ** -0.5))) * gate
        y = jnp.sum(coef * buf[slot, :, d:], axis=0, keepdims=True)
        o_ref[pl.ds(t, 1), :] = h_ref[pl.ds(t, 1), :] + res_gate_ref[0] * y

    def token_group(g, carry):
        for slot in range(nbuf):
            t = g * nbuf + slot
            start_token(idx_ref, t + nbuf - 1, (slot + nbuf - 1) % nbuf)
            finish_token(t, slot)
        return carry

    lax.fori_loop(0, tb // nbuf - 1, token_group, 0)
    for slot in range(nbuf):
        t = tb - nbuf + slot
        ahead = t + nbuf - 1
        if ahead < tb:
            start_token(idx_ref, ahead, ahead % nbuf)
        else:
            start_token(idx_next_ref, ahead - tb, ahead % nbuf)
        finish_token(t, slot)

    @pl.when(step == last_step)
    def _():
        for slot in range(nbuf - 1):
            wait_token(slot)


def _peer_experts(xt, idx_t, gate_t, h, res_gate, u_tab, v_tab):
    T, D = xt.shape
    tb = PEER_TOKENS_PER_STEP
    nsteps = T // tb
    steps_per_batch = nsteps // res_gate.shape[0]
    assert T % LANES == 0 and LANES % tb == 0 and tb % PEER_GATHER_BUFFERS == 0
    uv = jnp.concatenate([u_tab, v_tab], axis=1)[:, None, :]
    steps_per_gate_tile = LANES // tb
    idx = idx_t.T
    return pl.pallas_call(
        _peer_expert_kernel,
        grid=(nsteps,),
        in_specs=[pl.BlockSpec((tb, PEER_SLOTS), lambda i: (i, 0), memory_space=pltpu.SMEM),
                  pl.BlockSpec((tb, PEER_SLOTS), lambda i: (jnp.minimum(i + 1, nsteps - 1), 0),
                               memory_space=pltpu.SMEM),
                  pl.BlockSpec((tb, D), lambda i: (i, 0)),
                  pl.BlockSpec((PEER_SLOTS, LANES), lambda i: (0, i // steps_per_gate_tile)),
                  pl.BlockSpec((tb, D), lambda i: (i, 0)),
                  pl.BlockSpec((1, 1, D), lambda i: (i // steps_per_batch, 0, 0)),
                  pl.BlockSpec(memory_space=pl.ANY)],
        out_specs=pl.BlockSpec((tb, D), lambda i: (i, 0)),
        out_shape=jax.ShapeDtypeStruct((T, D), xt.dtype),
        scratch_shapes=[pltpu.VMEM((PEER_GATHER_BUFFERS, PEER_SLOTS, 2 * D), jnp.float32),
                        pltpu.SemaphoreType.DMA((PEER_GATHER_BUFFERS,))],
        compiler_params=pltpu.CompilerParams(dimension_semantics=("arbitrary",)),
        name="peer_experts",
    )(idx, idx, xt, gate_t, h, res_gate, uv)


ROUTE_TOKENS_PER_STEP = 256
assert PEER_TOPK == 16


def _top_rows(s, order, payload, n):
    vals, outs = [], []
    for _ in range(n):
        m = jnp.max(s, axis=0, keepdims=True)
        first = jnp.min(jnp.where(s == m, order, jnp.inf), axis=0, keepdims=True)
        hit = order == first
        vals.append(m)
        outs.append(first if payload is None else jnp.max(jnp.where(hit, payload, -1.0), axis=0, keepdims=True))
        s = jnp.where(hit, -jnp.inf, s)
    return jnp.concatenate(vals, axis=0), jnp.concatenate(outs, axis=0)


def _candidate_blocks(s1, i1, s2, i2):
    k = PEER_TOPK
    c = s1.shape[1]
    r8 = lax.broadcasted_iota(jnp.int32, (8, c), 0).astype(jnp.float32)
    r16 = lax.broadcasted_iota(jnp.int32, (k, c), 0).astype(jnp.float32)
    ninf = -jnp.inf

    def col(i, rows, r, keep):
        return (jnp.where(keep, s1[i:i + 1] + s2[:rows], ninf), i * k + r, i1[i:i + 1] * PEER_NKEYS + i2[:rows])

    def row(j, rows, r, keep):
        return (jnp.where(keep, s1[:rows] + s2[j:j + 1], ninf), r * k + j, i1[:rows] * PEER_NKEYS + i2[j:j + 1])

    blocks = [row(0, k, r16, r16 >= 0), col(0, k, r16, r16 >= 1), col(1, 8, r8, r8 >= 1), row(1, 8, r8, r8 >= 2),
              row(2, 8, r8, (r8 >= 2) & (r8 <= 4)), row(3, 8, r8, (r8 >= 2) & (r8 <= 3)), row(4, 8, r8, r8 == 2)]
    return [jnp.concatenate([b[n] for b in blocks], axis=0) for n in range(3)]


def _peer_route_kernel(h_ref, g_ref, sh_ref, sc_ref, wq_ref, keys_ref, xn_ref, idx_ref, gate_ref):
    x = h_ref[...]
    xn = x * lax.rsqrt(jnp.mean(x * x, axis=-1, keepdims=True) + EPS) * g_ref[...]
    xm = xn * (1 + sc_ref[0]) + sh_ref[0]
    xn_ref[...] = xm
    xb = xm.astype(jnp.bfloat16)
    tb = x.shape[0]
    dk2 = PEER_DK // 2
    key_order = lax.broadcasted_iota(jnp.int32, (PEER_NKEYS, tb), 0).astype(jnp.float32)

    def head(h, carry):
        q = jnp.dot(xb, wq_ref[h], preferred_element_type=jnp.float32)
        tops = []
        for p in range(2):
            qp = q[:, p * dk2:(p + 1) * dk2].astype(jnp.bfloat16)
            s = lax.dot_general(keys_ref[h, p], qp, (((1,), (1,)), ((), ())),
                                preferred_element_type=jnp.float32)
            tops.append(_top_rows(s, key_order, None, PEER_TOPK))
        (s1, i1), (s2, i2) = tops
        cand, order, expert = _candidate_blocks(s1, i1, s2, i2)
        top_s, top_e = _top_rows(cand, order, expert, PEER_TOPK)
        e = jnp.exp(top_s - top_s[0:1])
        rows = pl.ds(pl.multiple_of(h * PEER_TOPK, PEER_TOPK), PEER_TOPK)
        gate_ref[rows, :] = e / jnp.sum(e, axis=0, keepdims=True)
        idx_ref[rows, :] = top_e.astype(jnp.int32)
        return carry

    lax.fori_loop(0, PEER_HEADS, head, 0)


def _peer_route(h, g, shift, scale, w_q, sub_keys):
    B, L, D = h.shape
    T = B * L
    tb = min(ROUTE_TOKENS_PER_STEP, L)
    assert L % tb == 0 and tb % LANES == 0
    steps_per_batch = L // tb
    dk2 = PEER_DK // 2
    wq = w_q.reshape(D, PEER_HEADS, 2 * dk2).transpose(1, 0, 2).astype(jnp.bfloat16)
    keys = sub_keys.astype(jnp.bfloat16)
    return pl.pallas_call(
        _peer_route_kernel,
        grid=(T // tb,),
        in_specs=[pl.BlockSpec((tb, D), lambda i: (i, 0)),
                  pl.BlockSpec((1, D), lambda i: (0, 0)),
                  pl.BlockSpec((1, 1, D), lambda i: (i // steps_per_batch, 0, 0)),
                  pl.BlockSpec((1, 1, D), lambda i: (i // steps_per_batch, 0, 0)),
                  pl.BlockSpec((PEER_HEADS, D, 2 * dk2), lambda i: (0, 0, 0)),
                  pl.BlockSpec((PEER_HEADS, 2, PEER_NKEYS, dk2), lambda i: (0, 0, 0, 0))],
        out_specs=[pl.BlockSpec((tb, D), lambda i: (i, 0)),
                   pl.BlockSpec((PEER_SLOTS, tb), lambda i: (0, i)),
                   pl.BlockSpec((PEER_SLOTS, tb), lambda i: (0, i))],
        out_shape=[jax.ShapeDtypeStruct((T, D), h.dtype),
                   jax.ShapeDtypeStruct((PEER_SLOTS, T), jnp.int32),
                   jax.ShapeDtypeStruct((PEER_SLOTS, T), jnp.float32)],
        compiler_params=pltpu.CompilerParams(dimension_semantics=("arbitrary",),
                                             vmem_limit_bytes=48 * 1024 * 1024),
        name="peer_route",
    )(h.reshape(T, D), g.reshape(1, D), shift, scale, wq, keys)


def _peer_residual(h, g, shift, scale, res_gate, w_q, sub_keys, u_tab, v_tab):
    B, L, D = h.shape
    xn, idx_t, gate_t = _peer_route(h, g, shift, scale, w_q, sub_keys)
    return _peer_experts(xn, idx_t, gate_t, h.reshape(B * L, D), res_gate, u_tab, v_tab).reshape(B, L, D)


def _final_rmsnorm_kernel(h_ref, g_ref, o_ref):
    x = h_ref[...]
    y = x * lax.rsqrt(jnp.mean(x * x, axis=-1, keepdims=True) + EPS)
    o_ref[...] = y * g_ref[...]


def _final_rmsnorm(h, g):
    B, L, D = h.shape
    T = B * L
    tm = 1024
    out = pl.pallas_call(
        _final_rmsnorm_kernel,
        grid=(T // tm,),
        in_specs=[pl.BlockSpec((tm, D), lambda i: (i, 0)),
                  pl.BlockSpec((1, D), lambda i: (0, 0))],
        out_specs=pl.BlockSpec((tm, D), lambda i: (i, 0)),
        out_shape=jax.ShapeDtypeStruct((T, D), h.dtype),
        name="final_rmsnorm",
    )(h.reshape(T, D), g.reshape(1, D))
    return out.reshape(B, L, D)


def kernel(x, c, ctx, c_ctx, ada_w, ada_b, norm_g,
           dn_w_in, dn_conv_w, dn_a_log, dn_dt_bias, dn_norm_g, dn_w_out,
           hy_w_in, hy_conv_w, hy_conv_b, hy_f_w1, hy_f_b1, hy_f_w2, hy_f_b2,
           hy_f_w3, hy_f_b3, hy_f_w4, hy_freq, hy_skip, hy_w_out,
           peer_w_q, peer_keys, peer_u, peer_v, final_g):
    f32 = jnp.float32
    B, L, D = x.shape
    silu_c = jax.nn.silu(c.astype(f32))
    silu_cc = jax.nn.silu(c_ctx.astype(f32))
    h, hc = x, ctx
    for i in range(DEPTH):
        kind = i % N_MIXERS
        j = i // N_MIXERS
        assert not any(l % N_MIXERS == MIXER_DELTA for l in range(i + 1, DEPTH))
        w_ada = ada_w[i].astype(f32)
        b_ada = ada_b[i].astype(f32)
        mod = (silu_c @ w_ada + b_ada).astype(h.dtype).reshape(B, 6, 1, D)
        sh1, sc1, gt1, sh2, sc2, gt2 = [mod[:, m] for m in range(6)]
        if kind == MIXER_DELTA:
            modc = (silu_cc @ w_ada + b_ada).astype(hc.dtype).reshape(6, 1, 1, D)
            h = _deltanet_residual(h, hc, norm_g[i, 0], sh1, sc1, modc[0], modc[1], gt1, dn_w_in[j], dn_conv_w[j],
                                   dn_a_log[j], dn_dt_bias[j], dn_norm_g[j], dn_w_out[j])
        else:
            hy_args = (hy_w_in[j], hy_conv_w[j], hy_conv_b[j], hy_f_w1[j], hy_f_b1[j], hy_f_w2[j],
                       hy_f_b2[j], hy_f_w3[j], hy_f_b3[j], hy_f_w4[j], hy_freq[j], hy_skip[j], hy_w_out[j])
            h = _hyena_residual(h, norm_g[i, 0], sh1, sc1, gt1, *hy_args)
        peer_args = (peer_w_q[i], peer_keys[i], peer_u[i], peer_v[i])
        h = _peer_residual(h, norm_g[i, 1], sh2, sc2, gt2, *peer_args)
    return _final_rmsnorm(h, final_g)
```

```python
import cmath
import functools
import math
import jax
import jax.numpy as jnp
from jax import lax
from jax.experimental import pallas as pl
from jax.experimental.pallas import tpu as pltpu

D_MODEL = 1024
BATCH = 8
SEQ = 8192
DEPTH = 2

GRID_W = 64
CTX_LEN = 256

N_MIXERS = 2
MIXER_DELTA = 0
MIXER_HYENA = 1

EPS = 1e-6

DN_HEAD_DIM = 128
DN_HEADS = D_MODEL // DN_HEAD_DIM
DN_INNER = DN_HEADS * DN_HEAD_DIM
DN_CONV = 5
DN_CHUNK = 64
DN_CHUNK_LOG2 = 6
DN_PROJ = 4 * DN_INNER + 4 * DN_HEADS

HY_WIDTH = D_MODEL
HY_SHORT = 3
HY_BANDS = 16
HY_EMB = 1 + 2 * HY_BANDS
HY_FF = 64
HY_FAST = 0.3
HY_SLOW = 1.5
HY_TARGET = 1e-2

PEER_HEADS = 8
PEER_NKEYS = 128
PEER_EXPERTS = PEER_NKEYS * PEER_NKEYS
PEER_DK = 256
PEER_TOPK = 16


LANES = 128
V7X_VMEM_BYTES = 64 * 1024 * 1024
VMEM_LIMIT_BYTES = V7X_VMEM_BYTES * 3 // 4
FFT_VMEM_LIMIT_BYTES = V7X_VMEM_BYTES * 7 // 8
PROJ_TOKENS_PER_STEP = 256
BF16 = jnp.bfloat16


def _norm_modulate(h_ref, g_ref, sh_ref, sc_ref):
    x = h_ref[...]
    xn = x * lax.rsqrt(jnp.mean(x * x, axis=-1, keepdims=True) + EPS) * g_ref[...]
    return xn * (1 + sc_ref[0]) + sh_ref[0]


def _row_conv(z, w, pos, row_len):
    n, taps = z.shape[0], w.shape[0]
    acc = None
    for tap in range(taps):
        off = tap - (taps - 1) // 2
        zs = z if off == 0 else pltpu.roll(z, (-off) % n, axis=0)
        ok = (pos + off >= 0) & (pos + off < row_len)
        term = jnp.where(ok, zs, 0.0) * w[tap:tap + 1]
        acc = term if acc is None else acc + term
    return acc


def _chunk_scan(x, pos, reverse):
    n = x.shape[0]
    s = 1
    while s < DN_CHUNK:
        if reverse:
            x = x + jnp.where(pos < DN_CHUNK - s, pltpu.roll(x, n - s, axis=0), 0.0)
        else:
            x = x + jnp.where(pos >= s, pltpu.roll(x, s, axis=0), 0.0)
        s *= 2
    return x


def _dn_inproj_kernel(h_ref, g_ref, sh_ref, sc_ref, w_ref, wb_ref, wa_ref, conv_ref, nea_ref, dtb_ref,
                      q_ref, k_ref, v_ref, gate_ref, beta_ref, pre_ref, suf_ref, tot_ref, *, row_len):
    xb = _norm_modulate(h_ref, g_ref, sh_ref, sc_ref).astype(BF16)
    tb = xb.shape[0]
    tok = lax.broadcasted_iota(jnp.int32, (tb, 1), 0)
    pos = tok % row_len
    for part, out_ref in enumerate((q_ref, k_ref, v_ref)):
        for head in range(DN_HEADS):
            c0 = part * DN_INNER + head * DN_HEAD_DIM
            z = jnp.dot(xb, w_ref[:, c0:c0 + DN_HEAD_DIM], preferred_element_type=jnp.float32)
            y = _row_conv(z, conv_ref[:, c0:c0 + DN_HEAD_DIM], pos, row_len)
            y = y * (1.0 / (1.0 + jnp.exp(-y)))
            if part < 2:
                y = y * lax.rsqrt(jnp.sum(y * y, axis=-1, keepdims=True) + EPS)
            if part == 0:
                y = y * (DN_HEAD_DIM ** -0.5)
            out_ref[:, head * DN_HEAD_DIM:(head + 1) * DN_HEAD_DIM] = y
    gate_ref[...] = jnp.dot(xb, w_ref[:, 3 * DN_INNER:], preferred_element_type=jnp.float32)
    zb = jnp.dot(xb, wb_ref[...], preferred_element_type=jnp.float32)
    za = jnp.dot(xb, wa_ref[...], preferred_element_type=jnp.float32) + dtb_ref[...]
    beta_ref[...] = 1.0 / (1.0 + jnp.exp(-zb))
    logdecay = nea_ref[...] * (jnp.maximum(za, 0.0) + jnp.log(1.0 + jnp.exp(-jnp.abs(za))))
    cpos = tok % DN_CHUNK
    pre = _chunk_scan(logdecay, cpos, False)
    suf = _chunk_scan(logdecay, cpos, True)
    pre_ref[...] = pre
    suf_ref[...] = suf
    tot_ref[...] = pre + suf - logdecay


def _dn_inproj(h, g, shift, scale, w_in, conv_w, a_log, dt_bias, row_len):
    B, L, D = h.shape
    T = B * L
    tb = min(PROJ_TOKENS_PER_STEP, L)
    assert L % tb == 0 and tb % row_len == 0 and tb % DN_CHUNK == 0
    steps_per_batch = L // tb
    nh2 = 2 * DN_HEADS
    lane_pad = lambda a: jnp.pad(a, ((0, 0), (0, LANES - nh2)))
    w_main = w_in[:, :4 * DN_INNER].astype(BF16)
    w_beta = lane_pad(w_in[:, 4 * DN_INNER:4 * DN_INNER + nh2]).astype(BF16)
    w_a = lane_pad(w_in[:, 4 * DN_INNER + nh2:]).astype(BF16)
    nea = lane_pad((-jnp.exp(a_log.astype(jnp.float32))).reshape(1, nh2))
    dtb = lane_pad(dt_bias.astype(jnp.float32).reshape(1, nh2))
    tok_spec = lambda n: pl.BlockSpec((tb, n), lambda i: (i, 0))
    full = lambda a: pl.BlockSpec(a.shape, lambda i: (0,) * a.ndim)
    mod_spec = pl.BlockSpec((1, 1, D), lambda i: (i // steps_per_batch, 0, 0))
    g2 = g.reshape(1, D)
    outs = pl.pallas_call(
        functools.partial(_dn_inproj_kernel, row_len=row_len),
        grid=(T // tb,),
        in_specs=[tok_spec(D), full(g2), mod_spec, mod_spec, full(w_main), full(w_beta), full(w_a),
                  full(conv_w), full(nea), full(dtb)],
        out_specs=[tok_spec(DN_INNER)] * 4 + [tok_spec(LANES)] * 4,
        out_shape=[jax.ShapeDtypeStruct((T, DN_INNER), jnp.float32)] * 4
        + [jax.ShapeDtypeStruct((T, LANES), jnp.float32)] * 4,
        compiler_params=pltpu.CompilerParams(dimension_semantics=("arbitrary",),
                                             vmem_limit_bytes=VMEM_LIMIT_BYTES),
        name="dn_inproj",
    )(h.reshape(T, D), g2, shift, scale, w_main, w_beta, w_a, conv_w, nea, dtb)
    q, k, v, gate, beta, pre, suf, tot = outs
    H = DN_HEADS
    zero = jnp.zeros((T, H), jnp.float32)
    fwd, bwd = slice(0, H), slice(H, 2 * H)
    rows = jnp.stack([beta[:, fwd], beta[:, bwd], pre[:, fwd], suf[:, bwd], tot[:, fwd], tot[:, bwd], zero, zero],
                     axis=0)
    return q, k, v, gate, rows.transpose(2, 0, 1)


DN_GROUP = 2 * DN_CHUNK
DN_SCAN_TOKENS_PER_STEP = 1024
NEG_BIG = -1e30


def _dn_groups(ks, vs, qs, rowss, dirs, states):
    n = DN_GROUP
    each = lambda f, *cols: [f(*a) for a in zip(*cols)]
    dot = functools.partial(jnp.dot, preferred_element_type=jnp.float32)
    dot_t = lambda a, b: lax.dot_general(a, b, (((1,), (1,)), ((), ())), preferred_element_type=jnp.float32)
    bf = lambda x: x.astype(BF16)
    ii = lax.broadcasted_iota(jnp.int32, (n, n), 0)
    jj = lax.broadcasted_iota(jnp.int32, (n, n), 1)
    same = (ii // DN_CHUNK) == (jj // DN_CHUNK)
    eye = (ii == jj).astype(jnp.float32)
    incl = [same & ((jj >= ii) if d else (jj <= ii)) for d in dirs]
    strict = [same & ((jj > ii) if d else (jj < ii)) for d in dirs]
    pad = jnp.zeros((n - SUBLANES, n), jnp.float32)
    cols = each(lambda r: jnp.concatenate([r, pad], axis=0).T, rowss)
    beta_c = each(lambda c, d: c[:, d:d + 1], cols, dirs)
    g_c = each(lambda c, d: c[:, 2 + d:3 + d], cols, dirs)
    gtot_c = each(lambda c, d: c[:, 4 + d:5 + d], cols, dirs)
    g_r = each(lambda r, d: r[2 + d:3 + d, :], rowss, dirs)
    dmat = each(lambda m, gc, gr: jnp.exp(jnp.where(m, gc - gr, NEG_BIG)), incl, g_c, g_r)
    kb = each(lambda k, b: k * b, ks, beta_c)
    k16 = each(bf, ks)
    kk = each(lambda a, b: dot_t(bf(a), b), kb, k16)
    a = each(lambda m, x, dm: jnp.where(m, x * dm, 0.0), strict, kk, dmat)
    t = each(lambda x: eye - x, a)
    p = a
    for _ in range(DN_CHUNK_LOG2 - 1):
        p = each(lambda x: dot(bf(x), bf(x)), p)
        t = each(lambda x, y: dot(bf(x), bf(eye + y)), t, p)
    t16 = each(bf, t)
    eg = each(jnp.exp, g_c)
    u = each(lambda tm, v, b: dot(tm, bf(v * b)), t16, vs, beta_c)
    w = each(lambda tm, x, e: bf(dot(tm, bf(x * e))), t16, kb, eg)
    kd_t = each(lambda k, gt, gc: (k * jnp.exp(gt - gc)).T, ks, gtot_c, g_c)
    qa = each(lambda q, k, dm: bf(dot_t(bf(q), k) * dm), qs, k16, dmat)
    qg = each(lambda q, e: bf(q * e), qs, eg)
    lane = lax.broadcasted_iota(jnp.int32, (ks[0].shape[1], n), 1) // DN_CHUNK
    row = lax.broadcasted_iota(jnp.int32, (n, vs[0].shape[1]), 0) // DN_CHUNK
    v_new = [jnp.zeros_like(x) for x in u]
    outs = [[None, None] for _ in dirs]
    for step in range(2):
        cs = [(1 - step) if d else step for d in dirs]
        rs = [slice(c * DN_CHUNK, (c + 1) * DN_CHUNK) for c in cs]
        s16 = each(bf, states)
        vn = each(lambda x, y, s, r: x[r] - dot(y[r], s), u, w, s16, rs)
        v_new = each(lambda x, old, c: jnp.where(row == c, jnp.concatenate([x, x], axis=0), old), vn, v_new, cs)
        vn16 = each(bf, v_new)
        o = each(lambda x, y, s, z, r: dot(x[r], s) + dot(y[r], z), qg, qa, s16, vn16, rs)
        for x, (c, oc) in enumerate(zip(cs, o)):
            outs[x][c] = oc
        states = each(lambda s, gt, c, kt, z: s * jnp.exp(gt[c * DN_CHUNK:c * DN_CHUNK + 1, :])
                      + dot(bf(jnp.where(lane == c, kt, 0.0)), z), states, gtot_c, cs, kd_t, vn16)
    return states, [jnp.concatenate(o, axis=0) for o in outs]


def _dn_scan_kernel(kf_ref, vf_ref, qf_ref, rf_ref, kb_ref, vb_ref, qb_ref, rb_ref, s0_ref,
                    of_ref, ob_ref, sfin_ref, state):
    i = pl.program_id(2)
    tl = kf_ref.shape[0]
    n_groups = tl // DN_GROUP
    hp = state.shape[0]
    dk = DN_HEAD_DIM

    @pl.when(i == 0)
    def _():
        state[...] = s0_ref[0]

    def group(gi, carry):
        rf = pl.ds(pl.multiple_of(gi * DN_GROUP, DN_GROUP), DN_GROUP)
        rb = pl.ds(pl.multiple_of((n_groups - 1 - gi) * DN_GROUP, DN_GROUP), DN_GROUP)
        ks, vs, qs, rowss, dirs = [], [], [], [], []
        for hh in range(hp):
            c = slice(hh * dk, (hh + 1) * dk)
            ks += [kf_ref[rf, c], kb_ref[rb, c]]
            vs += [vf_ref[rf, c], vb_ref[rb, c]]
            qs += [qf_ref[rf, c], qb_ref[rb, c]]
            rowss += [rf_ref[hh, :, rf], rb_ref[hh, :, rb]]
            dirs += [0, 1]
        states, outs = _dn_groups(ks, vs, qs, rowss, dirs, list(carry))
        for hh in range(hp):
            c = slice(hh * dk, (hh + 1) * dk)
            of_ref[rf, c] = outs[2 * hh]
            ob_ref[rb, c] = outs[2 * hh + 1]
        return tuple(states)

    init = tuple(state[hh, d] for hh in range(hp) for d in range(2))
    final = lax.fori_loop(0, n_groups, group, init)
    for hh in range(hp):
        for d in range(2):
            state[hh, d] = final[2 * hh + d]

    @pl.when(i == pl.num_programs(2) - 1)
    def _():
        for hh in range(hp):
            for d in range(2):
                sfin_ref[0, hh, d] = final[2 * hh + d]


DN_HEADS_PER_STEP = 4


def _dn_scan(q, k, v, rows, s0, B, L):
    T = B * L
    H, dk, hp = DN_HEADS, DN_HEAD_DIM, DN_HEADS_PER_STEP
    tl = min(DN_SCAN_TOKENS_PER_STEP, L)
    assert L % tl == 0 and tl % DN_GROUP == 0 and H % hp == 0
    nb = L // tl
    fwd = lambda b, h, i: (b * nb + i, h)
    bwd = lambda b, h, i: (b * nb + nb - 1 - i, h)
    tok = lambda m: pl.BlockSpec((tl, hp * dk), m)
    row_f = pl.BlockSpec((hp, 8, tl), lambda b, h, i: (h, 0, b * nb + i))
    row_b = pl.BlockSpec((hp, 8, tl), lambda b, h, i: (h, 0, b * nb + nb - 1 - i))
    st = pl.BlockSpec((1, hp, 2, dk, dk), lambda b, h, i: (b, h, 0, 0, 0))
    return pl.pallas_call(
        _dn_scan_kernel,
        grid=(B, H // hp, nb),
        in_specs=[tok(fwd), tok(fwd), tok(fwd), row_f, tok(bwd), tok(bwd), tok(bwd), row_b, st],
        out_specs=[tok(fwd), tok(bwd), st],
        out_shape=[jax.ShapeDtypeStruct((T, DN_INNER), jnp.float32)] * 2
        + [jax.ShapeDtypeStruct((B, H, 2, dk, dk), jnp.float32)],
        scratch_shapes=[pltpu.VMEM((hp, 2, dk, dk), jnp.float32)],
        compiler_params=pltpu.CompilerParams(dimension_semantics=("arbitrary", "arbitrary", "arbitrary"),
                                             vmem_limit_bytes=VMEM_LIMIT_BYTES),
        name="dn_scan",
    )(k, v, q, rows, k, v, q, rows, s0)


def _dn_outproj_kernel(of_ref, ob_ref, gate_ref, gn_ref, w_ref, h_ref, rg_ref, o_ref):
    parts = []
    for head in range(DN_HEADS):
        c = slice(head * DN_HEAD_DIM, (head + 1) * DN_HEAD_DIM)
        o = of_ref[:, c] + ob_ref[:, c]
        on = o * lax.rsqrt(jnp.mean(o * o, axis=-1, keepdims=True) + EPS) * gn_ref[...]
        gt = gate_ref[:, c]
        parts.append((on * (gt * (1.0 / (1.0 + jnp.exp(-gt))))).astype(BF16))
    y = jnp.concatenate(parts, axis=1)
    o_ref[...] = h_ref[...] + rg_ref[0] * jnp.dot(y, w_ref[...], preferred_element_type=jnp.float32)


def _dn_outproj(o_f, o_b, gate, g_norm, w_out, h, res_gate):
    T, D = h.shape
    tb = PROJ_TOKENS_PER_STEP
    steps_per_batch = T // res_gate.shape[0] // tb
    tok = pl.BlockSpec((tb, D), lambda i: (i, 0))
    gn = g_norm.reshape(1, DN_HEAD_DIM).astype(jnp.float32)
    w16 = w_out.astype(BF16)
    return pl.pallas_call(
        _dn_outproj_kernel,
        grid=(T // tb,),
        in_specs=[tok, tok, tok, pl.BlockSpec(gn.shape, lambda i: (0, 0)), pl.BlockSpec(w16.shape, lambda i: (0, 0)),
                  tok, pl.BlockSpec((1, 1, D), lambda i: (i // steps_per_batch, 0, 0))],
        out_specs=tok,
        out_shape=jax.ShapeDtypeStruct((T, D), h.dtype),
        compiler_params=pltpu.CompilerParams(dimension_semantics=("arbitrary",)),
        name="dn_outproj",
    )(o_f, o_b, gate, gn, w16, h, res_gate)


def _deltanet_residual(h, hc, g, shift, scale, shift_c, scale_c, res_gate,
                       w_in, conv_w, a_log, dt_bias, o_norm_g, w_out):
    B, L, D = h.shape
    Lc = hc.shape[1]
    sh_c = jnp.broadcast_to(shift_c, (B, 1, D))
    sc_c = jnp.broadcast_to(scale_c, (B, 1, D))
    qc, kc, vc, _, rows_c = _dn_inproj(hc, g, sh_c, sc_c, w_in, conv_w, a_log, dt_bias, row_len=Lc)
    zeros = jnp.zeros((B, DN_HEADS, 2, DN_HEAD_DIM, DN_HEAD_DIM), jnp.float32)
    _, _, s_ctx = _dn_scan(qc, kc, vc, rows_c, zeros, B, Lc)
    q, k, v, gate, rows = _dn_inproj(h, g, shift, scale, w_in, conv_w, a_log, dt_bias, row_len=GRID_W)
    o_f, o_b, _ = _dn_scan(q, k, v, rows, s_ctx, B, L)
    return _dn_outproj(o_f, o_b, gate, o_norm_g, w_out, h.reshape(B * L, D), res_gate).reshape(B, L, D)


def _hy_inproj_kernel(h_ref, g_ref, sh_ref, sc_ref, w_ref, conv_ref, bias_ref, v_ref, x1_ref, x2_ref, *, row_len):
    xb = _norm_modulate(h_ref, g_ref, sh_ref, sc_ref).astype(BF16)
    pos = lax.broadcasted_iota(jnp.int32, (xb.shape[0], 1), 0) % row_len
    for part, out_ref in enumerate((v_ref, x1_ref, x2_ref)):
        for tile in range(HY_WIDTH // LANES):
            c0 = part * HY_WIDTH + tile * LANES
            z = jnp.dot(xb, w_ref[:, c0:c0 + LANES], preferred_element_type=jnp.float32)
            y = _row_conv(z, conv_ref[:, c0:c0 + LANES], pos, row_len) + bias_ref[:, c0:c0 + LANES]
            out_ref[:, tile * LANES:(tile + 1) * LANES] = y


def _hy_inproj(h, g, shift, scale, w_in, conv_w, conv_b, row_len):
    B, L, D = h.shape
    T = B * L
    tb = min(PROJ_TOKENS_PER_STEP, L)
    assert L % tb == 0 and tb % row_len == 0
    steps_per_batch = L // tb
    tok_spec = lambda n: pl.BlockSpec((tb, n), lambda i: (i, 0))
    full = lambda a: pl.BlockSpec(a.shape, lambda i: (0,) * a.ndim)
    mod_spec = pl.BlockSpec((1, 1, D), lambda i: (i // steps_per_batch, 0, 0))
    g2, w16, bias = g.reshape(1, D), w_in.astype(BF16), conv_b.reshape(1, -1)
    return pl.pallas_call(
        functools.partial(_hy_inproj_kernel, row_len=row_len),
        grid=(T // tb,),
        in_specs=[tok_spec(D), full(g2), mod_spec, mod_spec, full(w16), full(conv_w), full(bias)],
        out_specs=[tok_spec(HY_WIDTH)] * 3,
        out_shape=[jax.ShapeDtypeStruct((T, HY_WIDTH), jnp.float32)] * 3,
        compiler_params=pltpu.CompilerParams(dimension_semantics=("arbitrary",),
                                             vmem_limit_bytes=VMEM_LIMIT_BYTES),
        name="hy_inproj",
    )(h.reshape(T, D), g2, shift, scale, w16, conv_w, bias)


def _residual_proj_kernel(z_ref, w_ref, h_ref, rg_ref, o_ref):
    o_ref[...] = h_ref[...] + rg_ref[0] * jnp.dot(z_ref[...].astype(BF16), w_ref[...],
                                                   preferred_element_type=jnp.float32)


def _residual_proj(z, w_out, h, res_gate):
    T, D = h.shape
    tb = PROJ_TOKENS_PER_STEP
    steps_per_batch = T // res_gate.shape[0] // tb
    w16 = w_out.astype(BF16)
    return pl.pallas_call(
        _residual_proj_kernel,
        grid=(T // tb,),
        in_specs=[pl.BlockSpec((tb, z.shape[1]), lambda i: (i, 0)), pl.BlockSpec(w16.shape, lambda i: (0, 0)),
                  pl.BlockSpec((tb, D), lambda i: (i, 0)),
                  pl.BlockSpec((1, 1, D), lambda i: (i // steps_per_batch, 0, 0))],
        out_specs=pl.BlockSpec((tb, D), lambda i: (i, 0)),
        out_shape=jax.ShapeDtypeStruct((T, D), h.dtype),
        compiler_params=pltpu.CompilerParams(dimension_semantics=("arbitrary",)),
        name="residual_proj",
    )(z, w16, h, res_gate)


SUBLANES = 8
FFT_P = 128
FFT_RA, FFT_RB = 8, 16
assert FFT_RA * FFT_RB == FFT_P
FFT_COLS_PER_STEP = 4


def _cmul_const(v, w):
    re, im = v
    wr, wi = round(w.real, 15), round(w.imag, 15)
    if wi == 0.0:
        return (re, im) if wr == 1.0 else ((-re, -im) if wr == -1.0 else (re * wr, im * wr))
    if wr == 0.0:
        return (-im, re) if wi == 1.0 else ((im, -re) if wi == -1.0 else (-im * wi, re * wi))
    return re * wr - im * wi, re * wi + im * wr


def _cmul(v, w):
    return v[0] * w[0] - v[1] * w[1], v[0] * w[1] + v[1] * w[0]


def _small_dft(xs, sign):
    n = len(xs)
    if n == 1:
        return xs
    even, odd = _small_dft(xs[0::2], sign), _small_dft(xs[1::2], sign)
    out = [None] * n
    for k in range(n // 2):
        e = even[k]
        t = None if odd[k] is None else _cmul_const(odd[k], cmath.exp(sign * 2j * math.pi * k / n))
        if t is None:
            out[k] = out[k + n // 2] = e
        elif e is None:
            out[k], out[k + n // 2] = t, (-t[0], -t[1])
        else:
            out[k] = (e[0] + t[0], e[1] + t[1])
            out[k + n // 2] = (e[0] - t[0], e[1] - t[1])
    return out


def _dft_p(load, store, tmp_re, tmp_im, inverse):
    sign = 1 if inverse else -1
    first, second = (FFT_RB, FFT_RA) if inverse else (FFT_RA, FFT_RB)
    for o in range(second):
        src = [load(FFT_RB * o + i) for i in range(first)] if inverse else [load(FFT_RB * i + o) for i in range(first)]
        for i, y in enumerate(_small_dft(src, sign)):
            b, c = (i, o) if inverse else (o, i)
            y = _cmul_const(y, cmath.exp(sign * 2j * math.pi * b * c / FFT_P))
            tmp_re[FFT_RB * c + b] = y[0]
            tmp_im[FFT_RB * c + b] = y[1]
    for o in range(first):
        idx = [FFT_RB * i + o for i in range(second)] if inverse else [FFT_RB * o + i for i in range(second)]
        for i, y in enumerate(_small_dft([(tmp_re[p], tmp_im[p]) for p in idx], sign)):
            store(FFT_RB * i + o if inverse else FFT_RB * o + i, y)


def _fft_cols_kernel(tw_ref, xr_ref, xi_ref, yr_ref, yi_ref, tmp_re, tmp_im, *, inverse):
    cols = yr_ref.shape[-3]
    col0 = pl.program_id(1) * cols

    def column(j, carry):
        if inverse:
            load = lambda n: (xr_ref[0, n, j], xi_ref[0, n, j])

            def store(n, y):
                if n < FFT_P // 2:
                    yr_ref[0, 0, n, j] = y[0]
                    yi_ref[0, 0, n, j] = y[1]
        else:
            n_in = xr_ref.shape[2]
            load = lambda n: (xr_ref[0, 0, n, j], xi_ref[0, 0, n, j]) if n < n_in else None

            def store(p, y):
                y = _cmul(y, (tw_ref[0, p, col0 + j], tw_ref[1, p, col0 + j]))
                yr_ref[0, p, j] = y[0]
                yi_ref[0, p, j] = y[1]
        _dft_p(load, store, tmp_re, tmp_im, inverse)
        return carry

    lax.fori_loop(0, cols, column, 0)


def _fft_rows_kernel(tw_ref, xr_ref, xi_ref, hr_ref, hi_ref, yr_ref, yi_ref, tmp_re, tmp_im, mid_re, mid_im, *,
                     convolve):
    rows = yr_ref.shape[1]
    row0 = pl.program_id(1) * rows

    def row(r, carry):
        load = lambda n: (xr_ref[0, r, n], xi_ref[0, r, n])
        if not convolve:
            def store(q, y):
                yr_ref[0, r, q] = y[0]
                yi_ref[0, r, q] = y[1]
            _dft_p(load, store, tmp_re, tmp_im, False)
            return carry

        def store_mid(q, y):
            y = _cmul(y, (hr_ref[0, r, q], hi_ref[0, r, q]))
            mid_re[q] = y[0]
            mid_im[q] = y[1]
        _dft_p(load, store_mid, tmp_re, tmp_im, False)

        def store(n, y):
            y = _cmul(y, (tw_ref[0, row0 + r, n], -tw_ref[1, row0 + r, n]))
            yr_ref[0, r, n] = y[0]
            yi_ref[0, r, n] = y[1]
        _dft_p(lambda q: (mid_re[q], mid_im[q]), store, tmp_re, tmp_im, True)
        return carry

    lax.fori_loop(0, rows, row, 0)


def _fft_twiddles():
    p = jnp.arange(FFT_P)
    k1 = p // FFT_RB + FFT_RA * (p % FFT_RB)
    ang = (2 * math.pi / (FFT_P * FFT_P)) * (k1[:, None] * jnp.arange(FFT_P)[None, :]).astype(jnp.float32)
    return jnp.stack([jnp.cos(ang), -jnp.sin(ang)]).astype(jnp.float32)


_TILE = (SUBLANES, LANES)
_FFT_PARAMS = dict(dimension_semantics=("arbitrary", "arbitrary"), vmem_limit_bytes=FFT_VMEM_LIMIT_BYTES)
_TMP = pltpu.VMEM((FFT_P,) + _TILE, jnp.float32)


def _fft_forward_cols(x, tw):
    G = x.shape[1]
    j = FFT_COLS_PER_STEP
    in_spec = lambda ri: pl.BlockSpec((1, 1, x.shape[2], j) + _TILE, lambda g, c: (ri, g, 0, c, 0, 0))
    out_spec = pl.BlockSpec((1, FFT_P, j) + _TILE, lambda g, c: (g, 0, c, 0, 0))
    out = jax.ShapeDtypeStruct((G, FFT_P, FFT_P) + _TILE, jnp.float32)
    return pl.pallas_call(
        functools.partial(_fft_cols_kernel, inverse=False),
        grid=(G, FFT_P // j),
        in_specs=[pl.BlockSpec(memory_space=pltpu.SMEM), in_spec(0), in_spec(1)],
        out_specs=[out_spec, out_spec], out_shape=[out, out], scratch_shapes=[_TMP, _TMP],
        compiler_params=pltpu.CompilerParams(**_FFT_PARAMS), name="fft_fwd_cols",
    )(tw, x, x)


def _fft_rows(yr, yi, tw, hr=None, hi=None):
    G = yr.shape[0]
    k = FFT_COLS_PER_STEP
    spec = pl.BlockSpec((1, k, FFT_P) + _TILE, lambda g, r: (g, r, 0, 0, 0))
    out = jax.ShapeDtypeStruct(yr.shape, jnp.float32)
    convolve = hr is not None
    if not convolve:
        hr, hi = yr, yi
    return pl.pallas_call(
        functools.partial(_fft_rows_kernel, convolve=convolve),
        grid=(G, FFT_P // k),
        in_specs=[pl.BlockSpec(memory_space=pltpu.SMEM), spec, spec, spec, spec],
        out_specs=[spec, spec], out_shape=[out, out], scratch_shapes=[_TMP] * 4,
        compiler_params=pltpu.CompilerParams(**_FFT_PARAMS), name="fft_rows",
    )(tw, yr, yi, hr, hi)


def _fft_inverse_cols(yr, yi, tw):
    G = yr.shape[0]
    j = FFT_COLS_PER_STEP
    in_spec = pl.BlockSpec((1, FFT_P, j) + _TILE, lambda g, c: (g, 0, c, 0, 0))
    part = jax.ShapeDtypeStruct((1, G, FFT_P // 2, FFT_P) + _TILE, jnp.float32)
    o_spec = pl.BlockSpec((1, 1, FFT_P // 2, j) + _TILE, lambda g, c: (0, g, 0, c, 0, 0))
    re, im = pl.pallas_call(
        functools.partial(_fft_cols_kernel, inverse=True),
        grid=(G, FFT_P // j),
        in_specs=[pl.BlockSpec(memory_space=pltpu.SMEM), in_spec, in_spec],
        out_specs=[o_spec, o_spec], out_shape=[part, part], scratch_shapes=[_TMP, _TMP],
        compiler_params=pltpu.CompilerParams(**_FFT_PARAMS), name="fft_inv_cols",
    )(tw, yr, yi)
    return jnp.concatenate([re, im], axis=0)


def _to_fft_layout(u):
    B, L, C = u.shape
    half, ch = B // 2, SUBLANES // (B // 2)
    G = C // (ch * LANES)
    assert 2 * L == FFT_P * FFT_P and half * ch == SUBLANES and G * ch * LANES == C
    t = u.reshape(2, half, L, G, ch, LANES).transpose(0, 3, 2, 1, 4, 5).reshape(2, G, L, SUBLANES, LANES)
    return t.reshape(2, G, FFT_P // 2, FFT_P, SUBLANES, LANES)


def _from_fft_layout(t, B, L, C):
    half, ch = B // 2, SUBLANES // (B // 2)
    G = C // (ch * LANES)
    return t.reshape(2, G, L, half, ch, LANES).transpose(0, 3, 2, 1, 4, 5).reshape(B, L, C)


def _filter_spectra(filt, tw, B):
    _, _, L, C = filt.shape
    half, ch = B // 2, SUBLANES // (B // 2)
    G = C // (ch * LANES)
    n = 2 * L
    kern = jnp.concatenate([filt[:, 0], jnp.zeros((2, 1, C), jnp.float32), jnp.flip(filt[:, 1, 1:], axis=1)], axis=1)
    t = kern.reshape(2, n, G, ch, LANES).transpose(2, 1, 0, 3, 4)
    t = jnp.pad(t, ((0, 0), (0, 0), (0, half - 2), (0, 0), (0, 0))).reshape(G, n, SUBLANES, LANES)
    x = jnp.stack([t, jnp.zeros_like(t)]).reshape(2, G, FFT_P, FFT_P, SUBLANES, LANES)
    fr, fi = _fft_rows(*_fft_forward_cols(x, tw), tw)
    spectra = []
    for order in range(2):
        pick = lambda a: jnp.broadcast_to(
            a.reshape(G, FFT_P, FFT_P, half, ch, LANES)[:, :, :, order:order + 1] * (1.0 / n),
            (G, FFT_P, FFT_P, half, ch, LANES)).reshape(G, FFT_P, FFT_P, SUBLANES, LANES)
        spectra.append((pick(fr), pick(fi)))
    return spectra


def _long_conv_fft(u, spectrum, tw):
    B, L, C = u.shape
    yr, yi = _fft_forward_cols(_to_fft_layout(u), tw)
    yr, yi = _fft_rows(yr, yi, tw, *spectrum)
    return _from_fft_layout(_fft_inverse_cols(yr, yi, tw), B, L, C)


def _hyena_residual(h, g, shift, scale, res_gate, w_in, conv_w, conv_b, f_w1, f_b1, f_w2, f_b2, f_w3, f_b3, f_w4,
                    freq, skip, w_out):
    B, L, D = h.shape
    v, x1, x2 = [a.reshape(B, L, HY_WIDTH) for a in _hy_inproj(h, g, shift, scale, w_in, conv_w, conv_b, GRID_W)]
    filt = _hyena_filters(L, f_w1, f_b1, f_w2, f_b2, f_w3, f_b3, f_w4, freq)
    tw = _fft_twiddles()
    spec1, spec2 = _filter_spectra(filt, tw, B)
    z1 = x1 * (_long_conv_fft(v, spec1, tw) + v * skip[0])
    z2 = x2 * (_long_conv_fft(z1, spec2, tw) + z1 * skip[1])
    return _residual_proj(z2.reshape(B * L, HY_WIDTH), w_out, h.reshape(B * L, D), res_gate).reshape(B, L, D)


def _hyena_filters(L, w1, b1, w2, b2, w3, b3, w4, freq):
    f32 = jnp.float32
    D = w4.shape[-1] // 4
    pos = jnp.arange(L, dtype=f32)
    t = pos / max(L - 1, 1)
    ang = (2 * math.pi * pos / L)[:, None] * jnp.linspace(1e-4, HY_BANDS - 1, HY_BANDS, dtype=f32)[None]
    z = jnp.concatenate([t[:, None], jnp.cos(ang), -jnp.sin(ang)], axis=-1)
    fr = freq.astype(f32)
    hid = jnp.sin(fr * (z @ w1.astype(f32) + b1.astype(f32)))
    hid = jnp.sin(fr * (hid @ w2.astype(f32) + b2.astype(f32)))
    hid = jnp.sin(fr * (hid @ w3.astype(f32) + b3.astype(f32)))
    h = hid @ w4.astype(f32)
    deltas = jnp.abs(jnp.linspace(math.log(HY_TARGET) / HY_FAST, math.log(HY_TARGET) / HY_SLOW, D, dtype=f32))
    decay = jnp.exp(-t[:, None] * deltas[None])
    return (h.reshape(L, 2, 2, D) * decay[:, None, None, :]).transpose(1, 2, 0, 3)


PEER_SLOTS = PEER_HEADS * PEER_TOPK
PEER_TOKENS_PER_STEP = 64
PEER_GATHER_BUFFERS = 8


def _peer_expert_kernel(idx_ref, idx_next_ref, x_ref, gt_ref, h_ref, res_gate_ref, uv_ref, o_ref, buf, sem):
    tb, d = x_ref.shape
    nbuf = buf.shape[0]
    step = pl.program_id(0)
    last_step = pl.num_programs(0) - 1

    def row_copy(row, j, slot):
        return pltpu.make_async_copy(uv_ref.at[row], buf.at[slot, pl.ds(j, 1)], sem.at[slot])

    def start_token(ids_ref, t, slot):
        for j in range(PEER_SLOTS):
            row_copy(ids_ref[t, j], j, slot).start(priority=j % 2)

    def wait_token(slot):
        for j in range(PEER_SLOTS):
            row_copy(0, j, slot).wait()

    @pl.when(step == 0)
    def _():
        for t in range(nbuf - 1):
            start_token(idx_ref, t, t)

    lane = lax.broadcasted_iota(jnp.int32, (PEER_SLOTS, LANES), 1)
    lane0 = (step * tb) % LANES

    def finish_token(t, slot):
        wait_token(slot)
        x_row = x_ref[pl.ds(t, 1), :]
        hid = jnp.sum(buf[slot, :, :d] * x_row, axis=-1, keepdims=True)
        gate = jnp.sum(jnp.where(lane == lane0 + t, gt_ref[...], 0.0), axis=-1, keepdims=True)
        coef = 0.5 * hid * (1.0 + lax.erf(hid * (2.0 ** -0.5))) * gate
        y = jnp.sum(coef * buf[slot, :, d:], axis=0, keepdims=True)
        o_ref[pl.ds(t, 1), :] = h_ref[pl.ds(t, 1), :] + res_gate_ref[0] * y

    def token_group(g, carry):
        for slot in range(nbuf):
            t = g * nbuf + slot
            start_token(idx_ref, t + nbuf - 1, (slot + nbuf - 1) % nbuf)
            finish_token(t, slot)
        return carry

    lax.fori_loop(0, tb // nbuf - 1, token_group, 0)
    for slot in range(nbuf):
        t = tb - nbuf + slot
        ahead = t + nbuf - 1
        if ahead < tb:
            start_token(idx_ref, ahead, ahead % nbuf)
        else:
            start_token(idx_next_ref, ahead - tb, ahead % nbuf)
        finish_token(t, slot)

    @pl.when(step == last_step)
    def _():
        for slot in range(nbuf - 1):
            wait_token(slot)


def _peer_experts(xt, idx_t, gate_t, h, res_gate, u_tab, v_tab):
    T, D = xt.shape
    tb = PEER_TOKENS_PER_STEP
    nsteps = T // tb
    steps_per_batch = nsteps // res_gate.shape[0]
    assert T % LANES == 0 and LANES % tb == 0 and tb % PEER_GATHER_BUFFERS == 0
    uv = jnp.concatenate([u_tab, v_tab], axis=1)[:, None, :]
    steps_per_gate_tile = LANES // tb
    idx = idx_t.T
    return pl.pallas_call(
        _peer_expert_kernel,
        grid=(nsteps,),
        in_specs=[pl.BlockSpec((tb, PEER_SLOTS), lambda i: (i, 0), memory_space=pltpu.SMEM),
                  pl.BlockSpec((tb, PEER_SLOTS), lambda i: (jnp.minimum(i + 1, nsteps - 1), 0),
                               memory_space=pltpu.SMEM),
                  pl.BlockSpec((tb, D), lambda i: (i, 0)),
                  pl.BlockSpec((PEER_SLOTS, LANES), lambda i: (0, i // steps_per_gate_tile)),
                  pl.BlockSpec((tb, D), lambda i: (i, 0)),
                  pl.BlockSpec((1, 1, D), lambda i: (i // steps_per_batch, 0, 0)),
                  pl.BlockSpec(memory_space=pl.ANY)],
        out_specs=pl.BlockSpec((tb, D), lambda i: (i, 0)),
        out_shape=jax.ShapeDtypeStruct((T, D), xt.dtype),
        scratch_shapes=[pltpu.VMEM((PEER_GATHER_BUFFERS, PEER_SLOTS, 2 * D), jnp.float32),
                        pltpu.SemaphoreType.DMA((PEER_GATHER_BUFFERS,))],
        compiler_params=pltpu.CompilerParams(dimension_semantics=("arbitrary",)),
        name="peer_experts",
    )(idx, idx, xt, gate_t, h, res_gate, uv)


ROUTE_TOKENS_PER_STEP = 256
ROUTE_HEADS_PER_ITER = 4
assert PEER_TOPK == 16


def _top_rows(ss, orders, payloads, n):
    vals, outs = [[] for _ in ss], [[] for _ in ss]
    for _ in range(n):
        ms = [jnp.max(s, axis=0, keepdims=True) for s in ss]
        firsts = [jnp.min(jnp.where(s == m, o, jnp.inf), axis=0, keepdims=True) for s, m, o in zip(ss, ms, orders)]
        hits = [o == f for o, f in zip(orders, firsts)]
        for x, (m, f, hit, p) in enumerate(zip(ms, firsts, hits, payloads)):
            vals[x].append(m)
            outs[x].append(f if p is None else jnp.max(jnp.where(hit, p, -1.0), axis=0, keepdims=True))
        ss = [jnp.where(hit, -jnp.inf, s) for hit, s in zip(hits, ss)]
    return [(jnp.concatenate(v, axis=0), jnp.concatenate(o, axis=0)) for v, o in zip(vals, outs)]


def _candidate_blocks(s1, i1, s2, i2):
    k = PEER_TOPK
    c = s1.shape[1]
    r8 = lax.broadcasted_iota(jnp.int32, (8, c), 0).astype(jnp.float32)
    r16 = lax.broadcasted_iota(jnp.int32, (k, c), 0).astype(jnp.float32)
    ninf = -jnp.inf

    def col(i, rows, r, keep):
        return (jnp.where(keep, s1[i:i + 1] + s2[:rows], ninf), i * k + r, i1[i:i + 1] * PEER_NKEYS + i2[:rows])

    def row(j, rows, r, keep):
        return (jnp.where(keep, s1[:rows] + s2[j:j + 1], ninf), r * k + j, i1[:rows] * PEER_NKEYS + i2[j:j + 1])

    blocks = [row(0, k, r16, r16 >= 0), col(0, k, r16, r16 >= 1), col(1, 8, r8, r8 >= 1), row(1, 8, r8, r8 >= 2),
              row(2, 8, r8, (r8 >= 2) & (r8 <= 4)), row(3, 8, r8, (r8 >= 2) & (r8 <= 3)), row(4, 8, r8, r8 == 2)]
    return [jnp.concatenate([b[n] for b in blocks], axis=0) for n in range(3)]


def _peer_route_kernel(h_ref, g_ref, sh_ref, sc_ref, wq_ref, keys_ref, xn_ref, idx_ref, gate_ref):
    x = h_ref[...]
    xn = x * lax.rsqrt(jnp.mean(x * x, axis=-1, keepdims=True) + EPS) * g_ref[...]
    xm = xn * (1 + sc_ref[0]) + sh_ref[0]
    xn_ref[...] = xm
    xb = xm.astype(jnp.bfloat16)
    tb = x.shape[0]
    dk2 = PEER_DK // 2
    key_order = lax.broadcasted_iota(jnp.int32, (PEER_NKEYS, tb), 0).astype(jnp.float32)

    def head_group(hg, carry):
        heads = [hg * ROUTE_HEADS_PER_ITER + x for x in range(ROUTE_HEADS_PER_ITER)]
        scores = []
        for h in heads:
            q = jnp.dot(xb, wq_ref[h], preferred_element_type=jnp.float32)
            for p in range(2):
                qp = q[:, p * dk2:(p + 1) * dk2].astype(jnp.bfloat16)
                scores.append(lax.dot_general(keys_ref[h, p], qp, (((1,), (1,)), ((), ())),
                                              preferred_element_type=jnp.float32))
        tops = _top_rows(scores, [key_order] * len(scores), [None] * len(scores), PEER_TOPK)
        cands = [_candidate_blocks(*tops[2 * x], *tops[2 * x + 1]) for x in range(len(heads))]
        picks = _top_rows([c[0] for c in cands], [c[1] for c in cands], [c[2] for c in cands], PEER_TOPK)
        for h, (top_s, top_e) in zip(heads, picks):
            e = jnp.exp(top_s - top_s[0:1])
            rows = pl.ds(pl.multiple_of(h * PEER_TOPK, PEER_TOPK), PEER_TOPK)
            gate_ref[rows, :] = e / jnp.sum(e, axis=0, keepdims=True)
            idx_ref[rows, :] = top_e.astype(jnp.int32)
        return carry

    lax.fori_loop(0, PEER_HEADS // ROUTE_HEADS_PER_ITER, head_group, 0)


def _peer_route(h, g, shift, scale, w_q, sub_keys):
    B, L, D = h.shape
    T = B * L
    tb = min(ROUTE_TOKENS_PER_STEP, L)
    assert L % tb == 0 and tb % LANES == 0
    steps_per_batch = L // tb
    dk2 = PEER_DK // 2
    wq = w_q.reshape(D, PEER_HEADS, 2 * dk2).transpose(1, 0, 2).astype(jnp.bfloat16)
    keys = sub_keys.astype(jnp.bfloat16)
    return pl.pallas_call(
        _peer_route_kernel,
        grid=(T // tb,),
        in_specs=[pl.BlockSpec((tb, D), lambda i: (i, 0)),
                  pl.BlockSpec((1, D), lambda i: (0, 0)),
                  pl.BlockSpec((1, 1, D), lambda i: (i // steps_per_batch, 0, 0)),
                  pl.BlockSpec((1, 1, D), lambda i: (i // steps_per_batch, 0, 0)),
                  pl.BlockSpec((PEER_HEADS, D, 2 * dk2), lambda i: (0, 0, 0)),
                  pl.BlockSpec((PEER_HEADS, 2, PEER_NKEYS, dk2), lambda i: (0, 0, 0, 0))],
        out_specs=[pl.BlockSpec((tb, D), lambda i: (i, 0)),
                   pl.BlockSpec((PEER_SLOTS, tb), lambda i: (0, i)),
                   pl.BlockSpec((PEER_SLOTS, tb), lambda i: (0, i))],
        out_shape=[jax.ShapeDtypeStruct((T, D), h.dtype),
                   jax.ShapeDtypeStruct((PEER_SLOTS, T), jnp.int32),
                   jax.ShapeDtypeStruct((PEER_SLOTS, T), jnp.float32)],
        compiler_params=pltpu.CompilerParams(dimension_semantics=("arbitrary",),
                                             vmem_limit_bytes=VMEM_LIMIT_BYTES),
        name="peer_route",
    )(h.reshape(T, D), g.reshape(1, D), shift, scale, wq, keys)


def _peer_residual(h, g, shift, scale, res_gate, w_q, sub_keys, u_tab, v_tab):
    B, L, D = h.shape
    xn, idx_t, gate_t = _peer_route(h, g, shift, scale, w_q, sub_keys)
    return _peer_experts(xn, idx_t, gate_t, h.reshape(B * L, D), res_gate, u_tab, v_tab).reshape(B, L, D)


def _final_rmsnorm_kernel(h_ref, g_ref, o_ref):
    x = h_ref[...]
    y = x * lax.rsqrt(jnp.mean(x * x, axis=-1, keepdims=True) + EPS)
    o_ref[...] = y * g_ref[...]


def _final_rmsnorm(h, g):
    B, L, D = h.shape
    T = B * L
    tm = 1024
    out = pl.pallas_call(
        _final_rmsnorm_kernel,
        grid=(T // tm,),
        in_specs=[pl.BlockSpec((tm, D), lambda i: (i, 0)),
                  pl.BlockSpec((1, D), lambda i: (0, 0))],
        out_specs=pl.BlockSpec((tm, D), lambda i: (i, 0)),
        out_shape=jax.ShapeDtypeStruct((T, D), h.dtype),
        name="final_rmsnorm",
    )(h.reshape(T, D), g.reshape(1, D))
    return out.reshape(B, L, D)


def kernel(x, c, ctx, c_ctx, ada_w, ada_b, norm_g,
           dn_w_in, dn_conv_w, dn_a_log, dn_dt_bias, dn_norm_g, dn_w_out,
           hy_w_in, hy_conv_w, hy_conv_b, hy_f_w1, hy_f_b1, hy_f_w2, hy_f_b2,
           hy_f_w3, hy_f_b3, hy_f_w4, hy_freq, hy_skip, hy_w_out,
           peer_w_q, peer_keys, peer_u, peer_v, final_g):
    f32 = jnp.float32
    B, L, D = x.shape
    silu_c = jax.nn.silu(c.astype(f32))
    silu_cc = jax.nn.silu(c_ctx.astype(f32))
    h, hc = x, ctx
    for i in range(DEPTH):
        kind = i % N_MIXERS
        j = i // N_MIXERS
        assert not any(l % N_MIXERS == MIXER_DELTA for l in range(i + 1, DEPTH))
        w_ada = ada_w[i].astype(f32)
        b_ada = ada_b[i].astype(f32)
        mod = (silu_c @ w_ada + b_ada).astype(h.dtype).reshape(B, 6, 1, D)
        sh1, sc1, gt1, sh2, sc2, gt2 = [mod[:, m] for m in range(6)]
        if kind == MIXER_DELTA:
            modc = (silu_cc @ w_ada + b_ada).astype(hc.dtype).reshape(6, 1, 1, D)
            h = _deltanet_residual(h, hc, norm_g[i, 0], sh1, sc1, modc[0], modc[1], gt1, dn_w_in[j], dn_conv_w[j],
                                   dn_a_log[j], dn_dt_bias[j], dn_norm_g[j], dn_w_out[j])
        else:
            hy_args = (hy_w_in[j], hy_conv_w[j], hy_conv_b[j], hy_f_w1[j], hy_f_b1[j], hy_f_w2[j],
                       hy_f_b2[j], hy_f_w3[j], hy_f_b3[j], hy_f_w4[j], hy_freq[j], hy_skip[j], hy_w_out[j])
            h = _hyena_residual(h, norm_g[i, 0], sh1, sc1, gt1, *hy_args)
        peer_args = (peer_w_q[i], peer_keys[i], peer_u[i], peer_v[i])
        h = _peer_residual(h, norm_g[i, 1], sh2, sc2, gt2, *peer_args)
    return _final_rmsnorm(h, final_g)
```
